```python
import math
import jax
import jax.numpy as jnp
from jax import lax
import numpy as np

D_MODEL = 1024
BATCH = 8
SEQ = 4096
DEPTH = 1

HEAD_DIM = 64
A_Q_HEADS = 8
A_KV_HEADS = 2
A_WINDOW = 128
B_GROUPS = ((128, 1), (512, 4), (2048, 16))
B_HEADS_PER_GROUP = 8
A_WIDTH = A_Q_HEADS * HEAD_DIM
B_WIDTH = B_HEADS_PER_GROUP * HEAD_DIM
N_BRANCHES = 2
N_BUCKETS = 32
MAX_DISTANCE = 1024
REL_HEADS = A_Q_HEADS + len(B_GROUPS) * B_HEADS_PER_GROUP
PROJ_SIZES = (A_WIDTH, A_KV_HEADS * HEAD_DIM, A_KV_HEADS * HEAD_DIM,
              len(B_GROUPS) * B_WIDTH, len(B_GROUPS) * B_WIDTH, len(B_GROUPS) * B_WIDTH,
              A_WIDTH, B_WIDTH, N_BRANCHES * D_MODEL)
IN_WIDTH = sum(PROJ_SIZES)
EPS = 1e-6
NEG_INF = -1e30

kernel_name = "hybrid_gated_window_dilated_attention_block"


def rms_norm(x, gain):
    xf = x.astype(jnp.float32)
    y = xf * lax.rsqrt(jnp.mean(xf * xf, axis=-1, keepdims=True) + EPS)
    return (y * gain.astype(jnp.float32)).astype(x.dtype)


def t5_bucket(rel):
    half = N_BUCKETS // 2
    max_exact = half // 2
    ret = (rel > 0).astype(jnp.int32) * half
    n = jnp.abs(rel)
    nf = jnp.maximum(n, max_exact).astype(jnp.float32)
    large = max_exact + (jnp.log(nf / max_exact) / math.log(MAX_DISTANCE / max_exact)
                         * (half - max_exact)).astype(jnp.int32)
    large = jnp.minimum(large, half - 1)
    return ret + jnp.where(n < max_exact, n, large)


def banded_attention(q, k, v, half_window, stride, bias_table, sink):
    bsz, L, H, dh = q.shape
    KV = k.shape[2]
    G = H // KV
    blk = half_window
    nb = -(-L // blk)
    Lp = nb * blk
    pad = Lp - L
    qb = jnp.pad(q, ((0, 0), (0, pad), (0, 0), (0, 0))).reshape(bsz, nb, blk, KV, G, dh)

    def windows(t):
        tp = jnp.pad(t, ((0, 0), (blk, blk + pad), (0, 0), (0, 0))).reshape(bsz, nb + 2, blk, KV, dh)
        return jnp.concatenate([tp[:, :-2], tp[:, 1:-1], tp[:, 2:]], axis=2)

    kw = windows(k)
    vw = windows(v)
    scores = jnp.einsum('bnqkgd,bnmkd->bkgnqm', qb, kw).astype(jnp.float32)

    qi = jnp.arange(blk)
    mi = jnp.arange(3 * blk)
    rel = mi[None, :] - blk - qi[:, None]
    bias = bias_table[t5_bucket(rel * stride)]
    bias = bias.transpose(2, 0, 1).reshape(KV, G, 1, blk, 3 * blk).astype(jnp.float32)
    blocks = jnp.arange(nb)[:, None, None] * blk
    qpos = blocks + qi[None, :, None]
    kpos = blocks - blk + mi[None, None, :]
    valid = (kpos >= 0) & (kpos < L) & (jnp.abs(kpos - qpos) <= half_window)

    logits = jnp.where(valid, scores + bias, NEG_INF)
    lse = jax.nn.logsumexp(logits, axis=-1)
    if sink is not None:
        lse = jnp.logaddexp(lse, sink.astype(jnp.float32).reshape(KV, G, 1, 1))
    probs = jnp.exp(logits - lse[..., None])
    out = jnp.einsum('bkgnqm,bnmkd->bnqkgd', probs.astype(v.dtype), vw)
    out = out.reshape(bsz, Lp, H, dh)[:, :L]
    lse = lse.transpose(0, 3, 4, 1, 2).reshape(bsz, Lp, H)[:, :L]
    return out, lse


def dilated_group(q, k, v, window, dilation, bias_table):
    bsz, S, H, dh = q.shape
    L = S // dilation

    def to_sub(t):
        return t.reshape(bsz, L, dilation, H, dh).transpose(0, 2, 1, 3, 4).reshape(bsz * dilation, L, H, dh)

    out, lse = banded_attention(to_sub(q), to_sub(k), to_sub(v), window // (2 * dilation),
                                dilation, bias_table, None)
    out = out.reshape(bsz, dilation, L, H, dh).transpose(0, 2, 1, 3, 4).reshape(bsz, S, H, dh)
    lse = lse.reshape(bsz, dilation, L, H).transpose(0, 2, 1, 3).reshape(bsz, S, H)
    return out, lse


def head_rms_norm(t, gain):
    return rms_norm(t, gain)


def setup_inputs(seed: int = 0) -> dict:
    key = jax.random.key(seed)
    ks = jax.random.split(key, 16)
    f32 = jnp.float32
    x = jax.random.normal(ks[0], (BATCH, SEQ, D_MODEL), f32)
    norm_gain = 1.0 + 0.02 * jax.random.normal(ks[1], (DEPTH, D_MODEL), f32)
    w_in = jax.random.normal(ks[2], (DEPTH, D_MODEL, IN_WIDTH), f32) * D_MODEL ** -0.5
    q_norm_a = 1.0 + 0.02 * jax.random.normal(ks[3], (DEPTH, HEAD_DIM), f32)
    k_norm_a = 1.0 + 0.02 * jax.random.normal(ks[4], (DEPTH, HEAD_DIM), f32)
    q_norm_b = 1.0 + 0.02 * jax.random.normal(ks[5], (DEPTH, HEAD_DIM), f32)
    k_norm_b = 1.0 + 0.02 * jax.random.normal(ks[6], (DEPTH, HEAD_DIM), f32)
    sink_a = 0.5 * jax.random.normal(ks[7], (DEPTH, A_Q_HEADS), f32)
    rel_bias = 0.5 * jax.random.normal(ks[8], (N_BUCKETS, REL_HEADS), f32)
    w_branch_a = jax.random.normal(ks[9], (DEPTH, A_WIDTH, D_MODEL), f32) * A_WIDTH ** -0.5
    w_branch_b = jax.random.normal(ks[10], (DEPTH, B_WIDTH, D_MODEL), f32) * B_WIDTH ** -0.5
    b_merge = 0.1 * jax.random.normal(ks[11], (DEPTH, N_BRANCHES, D_MODEL), f32)
    w_out = jax.random.normal(ks[12], (DEPTH, D_MODEL, D_MODEL), f32) * D_MODEL ** -0.5
    return {"x": x, "norm_gain": norm_gain, "w_in": w_in, "q_norm_a": q_norm_a,
            "k_norm_a": k_norm_a, "q_norm_b": q_norm_b, "k_norm_b": k_norm_b,
            "sink_a": sink_a, "rel_bias": rel_bias, "w_branch_a": w_branch_a,
            "w_branch_b": w_branch_b, "b_merge": b_merge, "w_out": w_out}


def reference(x, norm_gain, w_in, q_norm_a, k_norm_a, q_norm_b, k_norm_b, sink_a,
              rel_bias, w_branch_a, w_branch_b, b_merge, w_out):
    bsz, S, D = x.shape
    n_groups = len(B_GROUPS)
    scale = HEAD_DIM ** -0.5
    split_idx = [int(i) for i in np.cumsum(PROJ_SIZES)[:-1]]
    for layer in range(DEPTH):
        h = rms_norm(x, norm_gain[layer])
        proj = h @ w_in[layer]
        qa, ka, va, qb, kb, vb, ga, gb, mg = jnp.split(proj, split_idx, axis=-1)

        qa = head_rms_norm(qa.reshape(bsz, S, A_Q_HEADS, HEAD_DIM), q_norm_a[layer]) * scale
        ka = head_rms_norm(ka.reshape(bsz, S, A_KV_HEADS, HEAD_DIM), k_norm_a[layer])
        va = va.reshape(bsz, S, A_KV_HEADS, HEAD_DIM)
        ya, _ = banded_attention(qa, ka, va, A_WINDOW, 1, rel_bias[:, :A_Q_HEADS], sink_a[layer])
        ya = ya.reshape(bsz, S, A_WIDTH) * jax.nn.silu(ga)

        qb = head_rms_norm(qb.reshape(bsz, S, n_groups, B_HEADS_PER_GROUP, HEAD_DIM), q_norm_b[layer]) * scale
        kb = head_rms_norm(kb.reshape(bsz, S, n_groups, B_HEADS_PER_GROUP, HEAD_DIM), k_norm_b[layer])
        vb = vb.reshape(bsz, S, n_groups, B_HEADS_PER_GROUP, HEAD_DIM)
        outs = []
        lses = []
        for g, (window, dilation) in enumerate(B_GROUPS):
            c0 = A_Q_HEADS + g * B_HEADS_PER_GROUP
            o, l = dilated_group(qb[:, :, g], kb[:, :, g], vb[:, :, g], window, dilation,
                                 rel_bias[:, c0:c0 + B_HEADS_PER_GROUP])
            outs.append(o)
            lses.append(l)
        alpha = jax.nn.softmax(jnp.stack(lses, axis=0), axis=0)
        yb = jnp.sum(alpha[..., None].astype(x.dtype) * jnp.stack(outs, axis=0), axis=0)
        yb = yb.reshape(bsz, S, B_WIDTH) * jax.nn.silu(gb)

        br_a = ya @ w_branch_a[layer]
        br_b = yb @ w_branch_b[layer]
        gates = jax.nn.sigmoid(mg.reshape(bsz, S, N_BRANCHES, D).astype(jnp.float32)
                               + b_merge[layer].astype(jnp.float32)).astype(x.dtype)
        merged = gates[:, :, 0] * br_a + gates[:, :, 1] * br_b
        x = x + merged @ w_out[layer]
    return x
```

```python
import functools
import math

import jax
import jax.numpy as jnp
import numpy as np
from jax import lax
from jax.experimental import pallas as pl
from jax.experimental.pallas import tpu as pltpu

D_MODEL = 1024
HEAD_DIM = 64
A_Q_HEADS = 8
A_KV_HEADS = 2
A_WINDOW = 128
B_GROUPS = ((128, 1), (512, 4), (2048, 16))
B_HEADS = 8
A_WIDTH = A_Q_HEADS * HEAD_DIM
B_WIDTH = B_HEADS * HEAD_DIM
N_BUCKETS = 32
MAX_DISTANCE = 1024
EPS = 1e-6
NEG_INF = -1e30

LANES = 128
PAIR = 2 * HEAD_DIM
Q_BLK = 128
VMEM_LIMIT = 56 * 1024 * 1024

BF16 = jnp.bfloat16
F32 = jnp.float32


def _head_norm(y, bd, gain):
    parts = []
    for c in range(0, y.shape[1], 256):
        yc = y[:, c:c + 256]
        ms = jnp.dot((yc * yc).astype(BF16), bd, preferred_element_type=F32)
        parts.append(yc * lax.rsqrt(ms + EPS))
    yn = parts[0] if len(parts) == 1 else jnp.concatenate(parts, axis=1)
    return yn * gain


def _silu(y):
    return y * (1.0 / (1.0 + jnp.exp(-y)))


def _in_proj_kernel(x_ref, ng_ref, w_ref, bd_ref, gqa_ref, gkva_ref, gqb_ref, gkb_ref, bm_ref,
                    qa_ref, ka_ref, va_ref, qb_refs, kb_refs, vb_refs, sga_ref, sgb_ref, gate_ref):
    x = x_ref[...]
    ms = jnp.mean(x * x, axis=-1, keepdims=True)
    h = ((x * lax.rsqrt(ms + EPS)) * ng_ref[...]).astype(BF16)
    bd = bd_ref[...]

    def proj(c0, width):
        return jnp.dot(h, w_ref[:, c0:c0 + width], preferred_element_type=F32)

    c = 0
    qa_ref[...] = _head_norm(proj(c, A_WIDTH), bd, gqa_ref[...]).astype(BF16)
    c += A_WIDTH
    y = proj(c, 2 * PAIR)
    ka_ref[...] = _head_norm(y, bd, gkva_ref[...])[:, :PAIR].astype(BF16)
    va_ref[...] = y[:, PAIR:].astype(BF16)
    c += 2 * PAIR
    for refs, gain_ref in ((qb_refs, gqb_ref), (kb_refs, gkb_ref), (vb_refs, None)):
        for g in range(len(B_GROUPS)):
            y = proj(c, B_WIDTH)
            if gain_ref is not None:
                y = _head_norm(y, bd, gain_ref[...])
            refs[g][...] = y.astype(BF16)
            c += B_WIDTH
    sga_ref[...] = _silu(proj(c, A_WIDTH)).astype(BF16)
    c += A_WIDTH
    sgb_ref[...] = _silu(proj(c, B_WIDTH)).astype(BF16)
    c += B_WIDTH
    for j in range(2 * D_MODEL // 512):
        y = proj(c, 512) + bm_ref[:, j * 512:(j + 1) * 512]
        gate_ref[:, j * 512:(j + 1) * 512] = (1.0 / (1.0 + jnp.exp(-y))).astype(BF16)
        c += 512


def _in_proj(x2, ng, w, bd, gqa, gkva, gqb, gkb, bm, tm):
    m = x2.shape[0]
    n_g = len(B_GROUPS)

    def rows(width):
        return pl.BlockSpec((tm, width), lambda i: (i, 0))

    def whole(a):
        return pl.BlockSpec(a.shape, lambda i: (0,) * a.ndim)

    def out(width):
        return jax.ShapeDtypeStruct((m, width), BF16)

    out_shape = (out(A_WIDTH), out(PAIR), out(PAIR),
                 [out(B_WIDTH)] * n_g, [out(B_WIDTH)] * n_g, [out(B_WIDTH)] * n_g,
                 out(A_WIDTH), out(B_WIDTH), out(2 * D_MODEL))
    out_specs = (rows(A_WIDTH), rows(PAIR), rows(PAIR),
                 [rows(B_WIDTH)] * n_g, [rows(B_WIDTH)] * n_g, [rows(B_WIDTH)] * n_g,
                 rows(A_WIDTH), rows(B_WIDTH), rows(2 * D_MODEL))
    return pl.pallas_call(
        _in_proj_kernel,
        grid=(m // tm,),
        in_specs=[rows(D_MODEL), whole(ng), pl.BlockSpec(memory_space=pltpu.VMEM),
                  whole(bd), whole(gqa), whole(gkva), whole(gqb), whole(gkb), whole(bm)],
        out_specs=out_specs,
        out_shape=out_shape,
        compiler_params=pltpu.CompilerParams(vmem_limit_bytes=VMEM_LIMIT),
        name="in_proj",
    )(x2, ng, w, bd, gqa, gkva, gqb, gkb, bm)


def _band_attn_kernel(*refs, seq_len, tq, half_window, n_cls, q_pairs, kv_pairs, has_sink, want_lse):
    it = iter(refs)
    q_ref, k_ref, v_ref, bias_ref = next(it), next(it), next(it), next(it)
    sink_ref = next(it) if has_sink else None
    o_ref = next(it)
    lse_ref = next(it) if want_lse else None

    tk = Q_BLK + 2 * half_window
    n_blk = seq_len // Q_BLK
    lane = lax.broadcasted_iota(jnp.int32, (1, PAIR), 1)
    lo = lane < HEAD_DIM
    t = pl.program_id(2)

    def block(j, carry):
        row = pl.multiple_of(j * Q_BLK, Q_BLK)
        blk = t * (tq // Q_BLK) + j
        ws = pl.multiple_of(jnp.clip(blk * Q_BLK - half_window, 0, seq_len - tk), HEAD_DIM)
        sel = (blk > 0).astype(jnp.int32) + (blk == n_blk - 1).astype(jnp.int32)
        for c in range(n_cls):
            lse_tile = jnp.zeros((Q_BLK, LANES), F32) if want_lse else None
            for p in range(q_pairs):
                kvp = p if kv_pairs == q_pairs else 0
                qcol = (c * q_pairs + p) * PAIR
                kcol = (c * kv_pairs + kvp) * PAIR
                q = q_ref[0, pl.ds(row, Q_BLK), qcol:qcol + PAIR]
                k = k_ref[0, pl.ds(ws, tk), kcol:kcol + PAIR]
                v = v_ref[0, pl.ds(ws, tk), kcol:kcol + PAIR]
                zero = jnp.zeros_like(q)
                acc = None
                inv = None
                for half in range(2):
                    head = 2 * p + half
                    keep = lo if half == 0 else jnp.logical_not(lo)
                    s = lax.dot_general(jnp.where(keep, q, zero), k, (((1,), (1,)), ((), ())),
                                        preferred_element_type=F32)
                    s = s + bias_ref[sel, head]
                    m = jnp.max(s, axis=-1, keepdims=True)
                    if has_sink:
                        m = jnp.maximum(m, sink_ref[head])
                    e = jnp.exp(s - m)
                    l = jnp.sum(e, axis=-1, keepdims=True)
                    if has_sink:
                        l = l + jnp.exp(sink_ref[head] - m)
                    pv = jnp.dot(e.astype(BF16), jnp.where(keep, v, jnp.zeros_like(v)),
                                 preferred_element_type=F32)
                    acc = pv if acc is None else acc + pv
                    r = 1.0 / l
                    inv = r if inv is None else jnp.where(lo, inv, r)
                    if want_lse:
                        lane_l = lax.broadcasted_iota(jnp.int32, (1, LANES), 1)
                        lse_tile = jnp.where(lane_l == head, m + jnp.log(l), lse_tile)
                o_ref[0, pl.ds(row, Q_BLK), qcol:qcol + PAIR] = (acc * inv).astype(BF16)
            if want_lse:
                lse_ref[0, pl.ds(row, Q_BLK), c * LANES:(c + 1) * LANES] = lse_tile
        return carry

    lax.fori_loop(0, tq // Q_BLK, block, 0)


def _band_attn(q, k, v, bias, sink, *, seq_len, tq, half_window, n_cls, want_lse):
    n_rows, _, q_cols = q.shape
    kv_cols = k.shape[2]
    heads = bias.shape[1]
    q_pairs = heads // 2
    n_col = q_cols // (q_pairs * PAIR)
    kv_pairs = kv_cols // (n_col * PAIR)
    tk = Q_BLK + 2 * half_window
    assert bias.shape == (3, heads, Q_BLK, tk)
    assert seq_len % tq == 0 and tq % Q_BLK == 0 and n_col % n_cls == 0

    q_blk = n_cls * q_pairs * PAIR
    kv_blk = n_cls * kv_pairs * PAIR
    in_specs = [pl.BlockSpec((1, tq, q_blk), lambda r, c, t: (r, t, c)),
                pl.BlockSpec((1, seq_len, kv_blk), lambda r, c, t: (r, 0, c)),
                pl.BlockSpec((1, seq_len, kv_blk), lambda r, c, t: (r, 0, c)),
                pl.BlockSpec(bias.shape, lambda r, c, t: (0, 0, 0, 0))]
    args = [q, k, v, bias]
    if sink is not None:
        in_specs.append(pl.BlockSpec(memory_space=pltpu.SMEM))
        args.append(sink)
    out_shape = [jax.ShapeDtypeStruct(q.shape, BF16)]
    out_specs = [pl.BlockSpec((1, tq, q_blk), lambda r, c, t: (r, t, c))]
    if want_lse:
        out_shape.append(jax.ShapeDtypeStruct((n_rows, seq_len, n_col * LANES), F32))
        out_specs.append(pl.BlockSpec((1, tq, n_cls * LANES), lambda r, c, t: (r, t, c)))
    kern = functools.partial(_band_attn_kernel, seq_len=seq_len, tq=tq, half_window=half_window,
                             n_cls=n_cls, q_pairs=q_pairs, kv_pairs=kv_pairs,
                             has_sink=sink is not None, want_lse=want_lse)
    return pl.pallas_call(
        kern,
        grid=(n_rows, n_col // n_cls, seq_len // tq),
        in_specs=in_specs,
        out_specs=out_specs,
        out_shape=out_shape,
        compiler_params=pltpu.CompilerParams(vmem_limit_bytes=VMEM_LIMIT),
        name=f"band_attn_hw{half_window}_len{seq_len}",
    )(*args)


def _merge_out_kernel(x_ref, oa_ref, sga_ref, ob_refs, lse_refs, sgb_ref, gate_ref,
                      expand_ref, wa_ref, wb_ref, wo_ref, y_ref):
    ya = (oa_ref[...].astype(F32) * sga_ref[...].astype(F32)).astype(BF16)
    lses = [r[...] for r in lse_refs]
    m = functools.reduce(jnp.maximum, lses)
    es = [jnp.exp(l - m) for l in lses]
    inv = 1.0 / functools.reduce(lambda a, b: a + b, es)
    yb = None
    for e, o_ref in zip(es, ob_refs):
        alpha = jnp.dot((e * inv).astype(BF16), expand_ref[...], preferred_element_type=F32)
        term = alpha * o_ref[...].astype(F32)
        yb = term if yb is None else yb + term
    yb = (yb * sgb_ref[...].astype(F32)).astype(BF16)
    br_a = jnp.dot(ya, wa_ref[...], preferred_element_type=F32)
    br_b = jnp.dot(yb, wb_ref[...], preferred_element_type=F32)
    merged = (gate_ref[:, :D_MODEL].astype(F32) * br_a
              + gate_ref[:, D_MODEL:].astype(F32) * br_b).astype(BF16)
    y_ref[...] = x_ref[...] + jnp.dot(merged, wo_ref[...], preferred_element_type=F32)


def _merge_out(x2, oa, sga, obs, lses, sgb, gates, expand, wa, wb, wo, tm):
    m = x2.shape[0]

    def rows(width):
        return pl.BlockSpec((tm, width), lambda i: (i, 0))

    def whole(a):
        return pl.BlockSpec(a.shape, lambda i: (0,) * a.ndim)

    return pl.pallas_call(
        _merge_out_kernel,
        grid=(m // tm,),
        in_specs=[rows(D_MODEL), rows(A_WIDTH), rows(A_WIDTH), [rows(B_WIDTH)] * len(obs),
                  [rows(LANES)] * len(lses), rows(B_WIDTH), rows(2 * D_MODEL),
                  whole(expand), whole(wa), whole(wb), whole(wo)],
        out_specs=rows(D_MODEL),
        out_shape=jax.ShapeDtypeStruct((m, D_MODEL), F32),
        compiler_params=pltpu.CompilerParams(vmem_limit_bytes=VMEM_LIMIT),
        name="merge_out",
    )(x2, oa, sga, obs, lses, sgb, gates, expand, wa, wb, wo)


def _t5_bucket(rel):
    half = N_BUCKETS // 2
    max_exact = half // 2
    ret = (rel > 0).astype(jnp.int32) * half
    n = jnp.abs(rel)
    nf = jnp.maximum(n, max_exact).astype(jnp.float32)
    large = max_exact + (jnp.log(nf / max_exact) / math.log(MAX_DISTANCE / max_exact)
                         * (half - max_exact)).astype(jnp.int32)
    large = jnp.minimum(large, half - 1)
    return ret + jnp.where(n < max_exact, n, large)


def _bias_tiles(table, half_window, stride):
    tk = Q_BLK + 2 * half_window
    qi = jnp.arange(Q_BLK)[:, None]
    kj = jnp.arange(tk)[None, :]
    tiles = []
    for lead in (0, half_window, 2 * half_window):
        rel = kj - lead - qi
        bias = table[_t5_bucket(rel * stride)].astype(F32)
        bias = jnp.where((jnp.abs(rel) <= half_window)[:, :, None], bias, NEG_INF)
        tiles.append(bias.transpose(2, 0, 1))
    return jnp.stack(tiles, axis=0)


_A_HEAD_ORDER = tuple(h for p in range(A_Q_HEADS // 2) for h in (p, p + A_Q_HEADS // 2))


def _head_cols(order):
    return np.concatenate([np.arange(h * HEAD_DIM, (h + 1) * HEAD_DIM) for h in order])


def kernel(x, norm_gain, w_in, q_norm_a, k_norm_a, q_norm_b, k_norm_b, sink_a, rel_bias,
           w_branch_a, w_branch_b, b_merge, w_out):
    bsz, seq, d = x.shape
    n_g = len(B_GROUPS)
    scale = HEAD_DIM ** -0.5
    a_cols = _head_cols(_A_HEAD_ORDER)

    bd = jnp.asarray(np.kron(np.eye(256 // HEAD_DIM), np.full((HEAD_DIM, HEAD_DIM), 1.0 / HEAD_DIM)), BF16)
    expand = jnp.asarray(np.kron(np.eye(LANES, B_HEADS), np.ones((1, HEAD_DIM))), BF16)
    for layer in range(norm_gain.shape[0]):
        c_ga = A_WIDTH + 2 * PAIR + 3 * n_g * B_WIDTH
        cols = np.arange(w_in.shape[2])
        cols[:A_WIDTH] = a_cols
        cols[c_ga:c_ga + A_WIDTH] = c_ga + a_cols
        w = w_in[layer][:, cols].astype(BF16)
        wa = w_branch_a[layer][a_cols].astype(BF16)
        wb = w_branch_b[layer].astype(BF16)
        wo = w_out[layer].astype(BF16)
        gqa = jnp.tile(q_norm_a[layer] * scale, A_WIDTH // HEAD_DIM)[None]
        gkva = jnp.tile(k_norm_a[layer], 2 * PAIR // HEAD_DIM)[None]
        gqb = jnp.tile(q_norm_b[layer] * scale, B_HEADS)[None]
        gkb = jnp.tile(k_norm_b[layer], B_HEADS)[None]
        bm = b_merge[layer].reshape(1, -1)
        sink = sink_a[layer][np.asarray(_A_HEAD_ORDER)]

        x2 = x.reshape(bsz * seq, d)
        qa, ka, va, qbs, kbs, vbs, sga, sgb, gates = _in_proj(
            x2, norm_gain[layer][None], w, bd, gqa, gkva, gqb, gkb, bm, tm=512)

        bias_a = _bias_tiles(rel_bias[:, :A_Q_HEADS][:, np.asarray(_A_HEAD_ORDER)], A_WINDOW, 1)
        (oa,) = _band_attn(qa.reshape(bsz, seq, A_WIDTH), ka.reshape(bsz, seq, PAIR),
                           va.reshape(bsz, seq, PAIR), bias_a, sink,
                           seq_len=seq, tq=1024, half_window=A_WINDOW, n_cls=1, want_lse=False)

        obs, lses = [], []
        for g, (window, dil) in enumerate(B_GROUPS):
            c0 = A_Q_HEADS + g * B_HEADS
            hw = window // (2 * dil)
            sub = seq // dil
            bias_g = _bias_tiles(rel_bias[:, c0:c0 + B_HEADS], hw, dil)
            view = (bsz, sub, dil * B_WIDTH)
            o_g, lse_g = _band_attn(qbs[g].reshape(view), kbs[g].reshape(view), vbs[g].reshape(view),
                                    bias_g, None, seq_len=sub, tq=min(sub, 1024), half_window=hw,
                                    n_cls=min(dil, 4), want_lse=True)
            obs.append(o_g.reshape(bsz * seq, B_WIDTH))
            lses.append(lse_g.reshape(bsz * seq, LANES))

        y2 = _merge_out(x2, oa.reshape(bsz * seq, A_WIDTH), sga, obs, lses, sgb, gates,
                        expand, wa, wb, wo, tm=512)
        x = y2.reshape(bsz, seq, d)
    return x
```

```python
import functools
import math

import jax
import jax.numpy as jnp
import numpy as np
from jax import lax
from jax.experimental import pallas as pl
from jax.experimental.pallas import tpu as pltpu

D_MODEL = 1024
HEAD_DIM = 64
A_Q_HEADS = 8
A_KV_HEADS = 2
A_WINDOW = 128
B_GROUPS = ((128, 1), (512, 4), (2048, 16))
B_HEADS = 8
A_WIDTH = A_Q_HEADS * HEAD_DIM
B_WIDTH = B_HEADS * HEAD_DIM
N_BUCKETS = 32
MAX_DISTANCE = 1024
EPS = 1e-6
NEG_INF = -1e30

LANES = 128
PAIR = 2 * HEAD_DIM
Q_BLK = 128
VMEM_LIMIT = 56 * 1024 * 1024

BF16 = jnp.bfloat16
F32 = jnp.float32


def _head_norm(y, bd, gain):
    parts = []
    for c in range(0, y.shape[1], 256):
        yc = y[:, c:c + 256]
        ms = jnp.dot((yc * yc).astype(BF16), bd, preferred_element_type=F32)
        parts.append(yc * lax.rsqrt(ms + EPS))
    yn = parts[0] if len(parts) == 1 else jnp.concatenate(parts, axis=1)
    return yn * gain


def _silu(y):
    return y * (1.0 / (1.0 + jnp.exp(-y)))


def _store_by_class(y, dil, scr_ref, out_ref):
    rows, width = y.shape
    if dil == 1:
        out_ref[...] = y.astype(out_ref.dtype)
        return
    slabs = width // LANES
    for s in range(slabs):
        scr_ref[s] = y[:, s * LANES:(s + 1) * LANES]
    for c in range(dil):
        for s in range(slabs):
            col = c * width + s * LANES
            out_ref[:, col:col + LANES] = scr_ref[s, pl.ds(c, rows // dil, stride=dil), :].astype(out_ref.dtype)


def _load_by_class(ref, dil, scr_ref, width):
    if dil == 1:
        return ref[...].astype(F32)
    sub_rows = ref.shape[0]
    slabs = width // LANES
    for c in range(dil):
        for s in range(slabs):
            col = c * width + s * LANES
            scr_ref[s, pl.ds(c, sub_rows, stride=dil), :] = ref[:, col:col + LANES].astype(F32)
    parts = [scr_ref[s] for s in range(slabs)]
    return parts[0] if slabs == 1 else jnp.concatenate(parts, axis=1)


def _in_proj_kernel(x_ref, ng_ref, w_ref, bd_ref, gqa_ref, gkva_ref, gqb_ref, gkb_ref, bm_ref,
                    qa_ref, ka_ref, va_ref, qb_refs, kb_refs, vb_refs, sga_ref, sgb_ref, gate_ref,
                    scr_refs):
    x = x_ref[...]
    ms = jnp.mean(x * x, axis=-1, keepdims=True)
    h = ((x * lax.rsqrt(ms + EPS)) * ng_ref[...]).astype(BF16)
    bd = bd_ref[...]

    def proj(c0, width):
        return jnp.dot(h, w_ref[:, c0:c0 + width], preferred_element_type=F32)

    c = 0
    qa_ref[...] = _head_norm(proj(c, A_WIDTH), bd, gqa_ref[...]).astype(BF16)
    c += A_WIDTH
    y = proj(c, 2 * PAIR)
    ka_ref[...] = _head_norm(y, bd, gkva_ref[...])[:, :PAIR].astype(BF16)
    va_ref[...] = y[:, PAIR:].astype(BF16)
    c += 2 * PAIR
    n_scr = 0
    for refs, gain_ref in ((qb_refs, gqb_ref), (kb_refs, gkb_ref), (vb_refs, None)):
        for g, (_, dil) in enumerate(B_GROUPS):
            y = proj(c, B_WIDTH)
            if gain_ref is not None:
                y = _head_norm(y, bd, gain_ref[...])
            _store_by_class(y, dil, scr_refs[n_scr % len(scr_refs)], refs[g])
            n_scr += dil > 1
            c += B_WIDTH
    sga_ref[...] = _silu(proj(c, A_WIDTH)).astype(BF16)
    c += A_WIDTH
    sgb_ref[...] = _silu(proj(c, B_WIDTH)).astype(BF16)
    c += B_WIDTH
    for j in range(2 * D_MODEL // 512):
        y = proj(c, 512) + bm_ref[:, j * 512:(j + 1) * 512]
        gate_ref[:, j * 512:(j + 1) * 512] = (1.0 / (1.0 + jnp.exp(-y))).astype(BF16)
        c += 512


def _in_proj(x2, ng, w, bd, gqa, gkva, gqb, gkb, bm, tm):
    m = x2.shape[0]
    n_g = len(B_GROUPS)

    def rows(width):
        return pl.BlockSpec((tm, width), lambda i: (i, 0))

    def whole(a):
        return pl.BlockSpec(a.shape, lambda i: (0,) * a.ndim)

    def out(width):
        return jax.ShapeDtypeStruct((m, width), BF16)

    b_shape = [jax.ShapeDtypeStruct((m // dil, dil * B_WIDTH), BF16) for _, dil in B_GROUPS]
    b_spec = [pl.BlockSpec((tm // dil, dil * B_WIDTH), lambda i: (i, 0)) for _, dil in B_GROUPS]
    out_shape = (out(A_WIDTH), out(PAIR), out(PAIR), b_shape, b_shape, b_shape,
                 out(A_WIDTH), out(B_WIDTH), out(2 * D_MODEL))
    out_specs = (rows(A_WIDTH), rows(PAIR), rows(PAIR), b_spec, b_spec, b_spec,
                 rows(A_WIDTH), rows(B_WIDTH), rows(2 * D_MODEL))
    scratch = [[pltpu.VMEM((B_WIDTH // LANES, tm, LANES), F32)] * 2]
    return pl.pallas_call(
        _in_proj_kernel,
        grid=(m // tm,),
        in_specs=[rows(D_MODEL), whole(ng), pl.BlockSpec(memory_space=pltpu.VMEM),
                  whole(bd), whole(gqa), whole(gkva), whole(gqb), whole(gkb), whole(bm)],
        out_specs=out_specs,
        out_shape=out_shape,
        scratch_shapes=scratch,
        compiler_params=pltpu.CompilerParams(vmem_limit_bytes=VMEM_LIMIT),
        name="in_proj",
    )(x2, ng, w, bd, gqa, gkva, gqb, gkb, bm)


def _band_attn_kernel(*refs, seq_len, tq, half_window, n_cls, q_pairs, kv_pairs, has_sink, want_lse):
    it = iter(refs)
    q_ref, k_ref, v_ref, bias_ref = next(it), next(it), next(it), next(it)
    sink_ref = next(it) if has_sink else None
    o_ref = next(it)
    lse_ref = next(it) if want_lse else None

    tk = Q_BLK + 2 * half_window
    n_blk = seq_len // Q_BLK
    lane = lax.broadcasted_iota(jnp.int32, (1, PAIR), 1)
    lo = lane < HEAD_DIM
    t = pl.program_id(2)

    def block(j, carry):
        row = pl.multiple_of(j * Q_BLK, Q_BLK)
        blk = t * (tq // Q_BLK) + j
        ws = pl.multiple_of(jnp.clip(blk * Q_BLK - half_window, 0, seq_len - tk), HEAD_DIM)
        sel = (blk > 0).astype(jnp.int32) + (blk == n_blk - 1).astype(jnp.int32)
        for c in range(n_cls):
            lse_tile = jnp.zeros((Q_BLK, LANES), F32) if want_lse else None
            for p in range(q_pairs):
                kvp = p if kv_pairs == q_pairs else 0
                qcol = (c * q_pairs + p) * PAIR
                kcol = (c * kv_pairs + kvp) * PAIR
                q = q_ref[0, pl.ds(row, Q_BLK), qcol:qcol + PAIR]
                k = k_ref[0, pl.ds(ws, tk), kcol:kcol + PAIR]
                v = v_ref[0, pl.ds(ws, tk), kcol:kcol + PAIR]
                zero = jnp.zeros_like(q)
                acc = None
                inv = None
                for half in range(2):
                    head = 2 * p + half
                    keep = lo if half == 0 else jnp.logical_not(lo)
                    s = lax.dot_general(jnp.where(keep, q, zero), k, (((1,), (1,)), ((), ())),
                                        preferred_element_type=F32)
                    s = s + bias_ref[sel, head]
                    m = jnp.max(s, axis=-1, keepdims=True)
                    if has_sink:
                        m = jnp.maximum(m, sink_ref[head])
                    e = jnp.exp(s - m)
                    l = jnp.sum(e, axis=-1, keepdims=True)
                    if has_sink:
                        l = l + jnp.exp(sink_ref[head] - m)
                    pv = jnp.dot(e.astype(BF16), jnp.where(keep, v, jnp.zeros_like(v)),
                                 preferred_element_type=F32)
                    acc = pv if acc is None else acc + pv
                    r = 1.0 / l
                    inv = r if inv is None else jnp.where(lo, inv, r)
                    if want_lse:
                        lane_l = lax.broadcasted_iota(jnp.int32, (1, LANES), 1)
                        lse_tile = jnp.where(lane_l == head, m + jnp.log(l), lse_tile)
                o_ref[0, pl.ds(row, Q_BLK), qcol:qcol + PAIR] = (acc * inv).astype(BF16)
            if want_lse:
                lse_ref[0, pl.ds(row, Q_BLK), c * LANES:(c + 1) * LANES] = lse_tile
        return carry

    lax.fori_loop(0, tq // Q_BLK, block, 0)


def _band_attn(q, k, v, bias, sink, *, seq_len, tq, half_window, n_cls, want_lse):
    n_rows, _, q_cols = q.shape
    kv_cols = k.shape[2]
    heads = bias.shape[1]
    q_pairs = heads // 2
    n_col = q_cols // (q_pairs * PAIR)
    kv_pairs = kv_cols // (n_col * PAIR)
    tk = Q_BLK + 2 * half_window
    assert bias.shape == (3, heads, Q_BLK, tk)
    assert seq_len % tq == 0 and tq % Q_BLK == 0 and n_col % n_cls == 0

    q_blk = n_cls * q_pairs * PAIR
    kv_blk = n_cls * kv_pairs * PAIR
    in_specs = [pl.BlockSpec((1, tq, q_blk), lambda r, c, t: (r, t, c)),
                pl.BlockSpec((1, seq_len, kv_blk), lambda r, c, t: (r, 0, c)),
                pl.BlockSpec((1, seq_len, kv_blk), lambda r, c, t: (r, 0, c)),
                pl.BlockSpec(bias.shape, lambda r, c, t: (0, 0, 0, 0))]
    args = [q, k, v, bias]
    if sink is not None:
        in_specs.append(pl.BlockSpec(memory_space=pltpu.SMEM))
        args.append(sink)
    out_shape = [jax.ShapeDtypeStruct(q.shape, BF16)]
    out_specs = [pl.BlockSpec((1, tq, q_blk), lambda r, c, t: (r, t, c))]
    if want_lse:
        out_shape.append(jax.ShapeDtypeStruct((n_rows, seq_len, n_col * LANES), F32))
        out_specs.append(pl.BlockSpec((1, tq, n_cls * LANES), lambda r, c, t: (r, t, c)))
    kern = functools.partial(_band_attn_kernel, seq_len=seq_len, tq=tq, half_window=half_window,
                             n_cls=n_cls, q_pairs=q_pairs, kv_pairs=kv_pairs,
                             has_sink=sink is not None, want_lse=want_lse)
    return pl.pallas_call(
        kern,
        grid=(n_rows, n_col // n_cls, seq_len // tq),
        in_specs=in_specs,
        out_specs=out_specs,
        out_shape=out_shape,
        compiler_params=pltpu.CompilerParams(vmem_limit_bytes=VMEM_LIMIT),
        name=f"band_attn_hw{half_window}_len{seq_len}",
    )(*args)


def _merge_out_kernel(x_ref, oa_ref, sga_ref, ob_refs, lse_refs, sgb_ref, gate_ref,
                      expand_ref, wa_ref, wb_ref, wo_ref, y_ref, scr_o_refs, scr_l_refs):
    ya = (oa_ref[...].astype(F32) * sga_ref[...].astype(F32)).astype(BF16)
    dils = [dil for _, dil in B_GROUPS]
    lses = [_load_by_class(r, dil, s, LANES) for r, dil, s in zip(lse_refs, dils, scr_l_refs)]
    m = functools.reduce(jnp.maximum, lses)
    es = [jnp.exp(l - m) for l in lses]
    inv = 1.0 / functools.reduce(lambda a, b: a + b, es)
    yb = None
    for e, o_ref, dil, scr in zip(es, ob_refs, dils, scr_o_refs):
        alpha = jnp.dot((e * inv).astype(BF16), expand_ref[...], preferred_element_type=F32)
        term = alpha * _load_by_class(o_ref, dil, scr, B_WIDTH)
        yb = term if yb is None else yb + term
    yb = (yb * sgb_ref[...].astype(F32)).astype(BF16)
    br_a = jnp.dot(ya, wa_ref[...], preferred_element_type=F32)
    br_b = jnp.dot(yb, wb_ref[...], preferred_element_type=F32)
    merged = (gate_ref[:, :D_MODEL].astype(F32) * br_a
              + gate_ref[:, D_MODEL:].astype(F32) * br_b).astype(BF16)
    y_ref[...] = x_ref[...] + jnp.dot(merged, wo_ref[...], preferred_element_type=F32)


def _merge_out(x2, oa, sga, obs, lses, sgb, gates, expand, wa, wb, wo, tm):
    m = x2.shape[0]

    def rows(width):
        return pl.BlockSpec((tm, width), lambda i: (i, 0))

    def whole(a):
        return pl.BlockSpec(a.shape, lambda i: (0,) * a.ndim)

    def by_class(width):
        return [pl.BlockSpec((tm // dil, dil * width), lambda i: (i, 0)) for _, dil in B_GROUPS]

    n_g = len(B_GROUPS)
    scratch = [[pltpu.VMEM((B_WIDTH // LANES, tm, LANES), F32)] * n_g,
               [pltpu.VMEM((1, tm, LANES), F32)] * n_g]
    return pl.pallas_call(
        _merge_out_kernel,
        grid=(m // tm,),
        in_specs=[rows(D_MODEL), rows(A_WIDTH), rows(A_WIDTH), by_class(B_WIDTH),
                  by_class(LANES), rows(B_WIDTH), rows(2 * D_MODEL),
                  whole(expand), whole(wa), whole(wb), whole(wo)],
        out_specs=rows(D_MODEL),
        out_shape=jax.ShapeDtypeStruct((m, D_MODEL), F32),
        scratch_shapes=scratch,
        compiler_params=pltpu.CompilerParams(vmem_limit_bytes=VMEM_LIMIT),
        name="merge_out",
    )(x2, oa, sga, obs, lses, sgb, gates, expand, wa, wb, wo)


def _t5_bucket(rel):
    half = N_BUCKETS // 2
    max_exact = half // 2
    ret = (rel > 0).astype(jnp.int32) * half
    n = jnp.abs(rel)
    nf = jnp.maximum(n, max_exact).astype(jnp.float32)
    large = max_exact + (jnp.log(nf / max_exact) / math.log(MAX_DISTANCE / max_exact)
                         * (half - max_exact)).astype(jnp.int32)
    large = jnp.minimum(large, half - 1)
    return ret + jnp.where(n < max_exact, n, large)


def _bias_tiles(table, half_window, stride):
    tk = Q_BLK + 2 * half_window
    qi = jnp.arange(Q_BLK)[:, None]
    kj = jnp.arange(tk)[None, :]
    lead = jnp.asarray([0, half_window, 2 * half_window])[:, None, None]
    rel = (kj - qi)[None] - lead
    bucket = _t5_bucket(rel * stride)[:, None]
    table = table.astype(F32)
    bias = jnp.zeros((3, table.shape[1], Q_BLK, tk), F32)
    for b in range(N_BUCKETS):
        bias = jnp.where(bucket == b, table[b][None, :, None, None], bias)
    return jnp.where((jnp.abs(rel) <= half_window)[:, None], bias, NEG_INF)


_A_HEAD_ORDER = tuple(h for p in range(A_Q_HEADS // 2) for h in (p, p + A_Q_HEADS // 2))


def kernel(x, norm_gain, w_in, q_norm_a, k_norm_a, q_norm_b, k_norm_b, sink_a, rel_bias,
           w_branch_a, w_branch_b, b_merge, w_out):
    bsz, seq, d = x.shape
    n_g = len(B_GROUPS)
    scale = HEAD_DIM ** -0.5

    bd = jnp.asarray(np.kron(np.eye(256 // HEAD_DIM), np.full((HEAD_DIM, HEAD_DIM), 1.0 / HEAD_DIM)), BF16)
    expand = jnp.asarray(np.kron(np.eye(LANES, B_HEADS), np.ones((1, HEAD_DIM))), BF16)
    for layer in range(norm_gain.shape[0]):
        c_ga = A_WIDTH + 2 * PAIR + 3 * n_g * B_WIDTH
        wl = w_in[layer]

        def pair_order(c0):
            return [wl[:, c0 + h * HEAD_DIM:c0 + (h + 1) * HEAD_DIM] for h in _A_HEAD_ORDER]

        w = jnp.concatenate(pair_order(0) + [wl[:, A_WIDTH:c_ga]] + pair_order(c_ga)
                            + [wl[:, c_ga + A_WIDTH:]], axis=1).astype(BF16)
        wa = jnp.concatenate([w_branch_a[layer][h * HEAD_DIM:(h + 1) * HEAD_DIM] for h in _A_HEAD_ORDER],
                             axis=0).astype(BF16)
        wb = w_branch_b[layer].astype(BF16)
        wo = w_out[layer].astype(BF16)
        gqa = jnp.tile(q_norm_a[layer] * scale, A_WIDTH // HEAD_DIM)[None]
        gkva = jnp.tile(k_norm_a[layer], 2 * PAIR // HEAD_DIM)[None]
        gqb = jnp.tile(q_norm_b[layer] * scale, B_HEADS)[None]
        gkb = jnp.tile(k_norm_b[layer], B_HEADS)[None]
        bm = b_merge[layer].reshape(1, -1)
        sink = sink_a[layer][np.asarray(_A_HEAD_ORDER)]

        x2 = x.reshape(bsz * seq, d)
        qa, ka, va, qbs, kbs, vbs, sga, sgb, gates = _in_proj(
            x2, norm_gain[layer][None], w, bd, gqa, gkva, gqb, gkb, bm, tm=512)

        bias_a = _bias_tiles(rel_bias[:, :A_Q_HEADS][:, np.asarray(_A_HEAD_ORDER)], A_WINDOW, 1)
        (oa,) = _band_attn(qa.reshape(bsz, seq, A_WIDTH), ka.reshape(bsz, seq, PAIR),
                           va.reshape(bsz, seq, PAIR), bias_a, sink,
                           seq_len=seq, tq=1024, half_window=A_WINDOW, n_cls=1, want_lse=False)

        obs, lses = [], []
        for g, (window, dil) in enumerate(B_GROUPS):
            c0 = A_Q_HEADS + g * B_HEADS
            hw = window // (2 * dil)
            sub = seq // dil
            bias_g = _bias_tiles(rel_bias[:, c0:c0 + B_HEADS], hw, dil)
            view = (bsz, sub, dil * B_WIDTH)
            o_g, lse_g = _band_attn(qbs[g].reshape(view), kbs[g].reshape(view), vbs[g].reshape(view),
                                    bias_g, None, seq_len=sub, tq=min(sub, 1024), half_window=hw,
                                    n_cls=min(dil, 4), want_lse=True)
            obs.append(o_g.reshape(bsz * sub, dil * B_WIDTH))
            lses.append(lse_g.reshape(bsz * sub, dil * LANES))

        y2 = _merge_out(x2, oa.reshape(bsz * seq, A_WIDTH), sga, obs, lses, sgb, gates,
                        expand, wa, wb, wo, tm=512)
        x = y2.reshape(bsz, seq, d)
    return x
```

```python
import functools
import math

import jax
import jax.numpy as jnp
import numpy as np
from jax import lax
from jax.experimental import pallas as pl
from jax.experimental.pallas import tpu as pltpu

D_MODEL = 1024
HEAD_DIM = 64
A_Q_HEADS = 8
A_KV_HEADS = 2
A_WINDOW = 128
B_GROUPS = ((128, 1), (512, 4), (2048, 16))
B_HEADS = 8
A_WIDTH = A_Q_HEADS * HEAD_DIM
B_WIDTH = B_HEADS * HEAD_DIM
N_BUCKETS = 32
MAX_DISTANCE = 1024
EPS = 1e-6
NEG_INF = -1e30

LANES = 128
PAIR = 2 * HEAD_DIM
Q_BLK = 128
VMEM_LIMIT = 56 * 1024 * 1024

BF16 = jnp.bfloat16
F32 = jnp.float32


def _head_norm(y, bd, gain):
    parts = []
    for c in range(0, y.shape[1], 256):
        yc = y[:, c:c + 256]
        ms = jnp.dot((yc * yc).astype(BF16), bd, preferred_element_type=F32)
        parts.append(yc * lax.rsqrt(ms + EPS))
    yn = parts[0] if len(parts) == 1 else jnp.concatenate(parts, axis=1)
    return yn * gain


def _silu(y):
    return y * (1.0 / (1.0 + jnp.exp(-y)))


def _store_by_class(y, dil, scr_ref, out_ref):
    rows, width = y.shape
    if dil == 1:
        out_ref[...] = y.astype(out_ref.dtype)
        return
    slabs = width // LANES
    for s in range(slabs):
        scr_ref[s] = y[:, s * LANES:(s + 1) * LANES]
    for c in range(dil):
        for s in range(slabs):
            col = c * width + s * LANES
            out_ref[:, col:col + LANES] = scr_ref[s, pl.ds(c, rows // dil, stride=dil), :].astype(out_ref.dtype)


def _load_by_class(ref, dil, scr_ref, width):
    if dil == 1:
        return ref[...].astype(F32)
    sub_rows = ref.shape[0]
    slabs = width // LANES
    for c in range(dil):
        for s in range(slabs):
            col = c * width + s * LANES
            scr_ref[s, pl.ds(c, sub_rows, stride=dil), :] = ref[:, col:col + LANES].astype(F32)
    parts = [scr_ref[s] for s in range(slabs)]
    return parts[0] if slabs == 1 else jnp.concatenate(parts, axis=1)


def _in_proj_kernel(x_ref, ng_ref, w_ref, bd_ref, gqa_ref, gkva_ref, gqb_ref, gkb_ref, bm_ref,
                    qa_ref, ka_ref, va_ref, qb_refs, kb_refs, vb_refs, sga_ref, sgb_ref, gate_ref,
                    scr_refs):
    x = x_ref[...]
    ms = jnp.mean(x * x, axis=-1, keepdims=True)
    h = ((x * lax.rsqrt(ms + EPS)) * ng_ref[...]).astype(BF16)
    bd = bd_ref[...]

    def proj(c0, width):
        return jnp.dot(h, w_ref[:, c0:c0 + width], preferred_element_type=F32)

    c = 0
    qa_ref[...] = _head_norm(proj(c, A_WIDTH), bd, gqa_ref[...]).astype(BF16)
    c += A_WIDTH
    y = proj(c, 2 * PAIR)
    ka_ref[...] = _head_norm(y, bd, gkva_ref[...])[:, :PAIR].astype(BF16)
    va_ref[...] = y[:, PAIR:].astype(BF16)
    c += 2 * PAIR
    n_scr = 0
    for refs, gain_ref in ((qb_refs, gqb_ref), (kb_refs, gkb_ref), (vb_refs, None)):
        for g, (_, dil) in enumerate(B_GROUPS):
            y = proj(c, B_WIDTH)
            if gain_ref is not None:
                y = _head_norm(y, bd, gain_ref[...])
            _store_by_class(y, dil, scr_refs[n_scr % len(scr_refs)], refs[g])
            n_scr += dil > 1
            c += B_WIDTH
    sga_ref[...] = _silu(proj(c, A_WIDTH)).astype(BF16)
    c += A_WIDTH
    sgb_ref[...] = _silu(proj(c, B_WIDTH)).astype(BF16)
    c += B_WIDTH
    for j in range(2 * D_MODEL // 512):
        y = proj(c, 512) + bm_ref[:, j * 512:(j + 1) * 512]
        gate_ref[:, j * 512:(j + 1) * 512] = (1.0 / (1.0 + jnp.exp(-y))).astype(BF16)
        c += 512


def _in_proj(x2, ng, w, bd, gqa, gkva, gqb, gkb, bm, tm):
    m = x2.shape[0]
    n_g = len(B_GROUPS)

    def rows(width):
        return pl.BlockSpec((tm, width), lambda i: (i, 0))

    def whole(a):
        return pl.BlockSpec(a.shape, lambda i: (0,) * a.ndim)

    def out(width):
        return jax.ShapeDtypeStruct((m, width), BF16)

    b_shape = [jax.ShapeDtypeStruct((m // dil, dil * B_WIDTH), BF16) for _, dil in B_GROUPS]
    b_spec = [pl.BlockSpec((tm // dil, dil * B_WIDTH), lambda i: (i, 0)) for _, dil in B_GROUPS]
    out_shape = (out(A_WIDTH), out(PAIR), out(PAIR), b_shape, b_shape, b_shape,
                 out(A_WIDTH), out(B_WIDTH), out(2 * D_MODEL))
    out_specs = (rows(A_WIDTH), rows(PAIR), rows(PAIR), b_spec, b_spec, b_spec,
                 rows(A_WIDTH), rows(B_WIDTH), rows(2 * D_MODEL))
    scratch = [[pltpu.VMEM((B_WIDTH // LANES, tm, LANES), F32)] * 2]
    return pl.pallas_call(
        _in_proj_kernel,
        grid=(m // tm,),
        in_specs=[rows(D_MODEL), whole(ng), pl.BlockSpec(memory_space=pltpu.VMEM),
                  whole(bd), whole(gqa), whole(gkva), whole(gqb), whole(gkb), whole(bm)],
        out_specs=out_specs,
        out_shape=out_shape,
        scratch_shapes=scratch,
        compiler_params=pltpu.CompilerParams(vmem_limit_bytes=VMEM_LIMIT),
        name="in_proj",
    )(x2, ng, w, bd, gqa, gkva, gqb, gkb, bm)


def _band_attn_kernel(*refs, seq_len, tq, half_window, n_cls, q_pairs, kv_pairs, stack, has_sink, want_lse):
    it = iter(refs)
    q_ref, k_ref, v_ref, bias_ref = next(it), next(it), next(it), next(it)
    sink_ref = next(it) if has_sink else None
    o_ref = next(it)
    lse_ref = next(it) if want_lse else None

    tk = Q_BLK + 2 * half_window
    n_blk = seq_len // Q_BLK
    lane = lax.broadcasted_iota(jnp.int32, (1, PAIR), 1)
    lo = lane < HEAD_DIM
    t = pl.program_id(2)

    def block(j, carry):
        row = pl.multiple_of(j * Q_BLK, Q_BLK)
        blk = t * (tq // Q_BLK) + j
        ws = pl.multiple_of(jnp.clip(blk * Q_BLK - half_window, 0, seq_len - tk), HEAD_DIM)
        sel = (blk > 0).astype(jnp.int32) + (blk == n_blk - 1).astype(jnp.int32)
        ones = jnp.ones((tk, PAIR), BF16)
        for c in range(n_cls):
            lse_tile = jnp.zeros((Q_BLK, LANES), F32) if want_lse else None
            for grp in range(q_pairs // stack):
                kcol = (c * kv_pairs + (grp if kv_pairs == q_pairs else 0)) * PAIR
                k = k_ref[0, pl.ds(ws, tk), kcol:kcol + PAIR]
                v = v_ref[0, pl.ds(ws, tk), kcol:kcol + PAIR]
                qcols = [(c * q_pairs + grp * stack + i) * PAIR for i in range(stack)]
                qs = []
                for qcol in qcols:
                    q = q_ref[0, pl.ds(row, Q_BLK), qcol:qcol + PAIR]
                    zero = jnp.zeros_like(q)
                    qs += [jnp.where(lo, q, zero), jnp.where(lo, zero, q)]
                s = lax.dot_general(jnp.concatenate(qs, axis=0), k, (((1,), (1,)), ((), ())),
                                    preferred_element_type=F32)
                s = s + bias_ref[sel, grp]
                m = jnp.max(s, axis=-1, keepdims=True)
                if has_sink:
                    m = jnp.maximum(m, sink_ref[...])
                e = jnp.exp(s - m).astype(BF16)
                pv = jnp.dot(e, jnp.concatenate([v, ones], axis=1), preferred_element_type=F32)
                l = pv[:, PAIR:]
                if has_sink:
                    l = l + jnp.exp(sink_ref[...] - m)
                out = pv[:, :PAIR] * (1.0 / l)
                if want_lse:
                    lse_rows = m + jnp.log(l)
                    lane_l = lax.broadcasted_iota(jnp.int32, (1, LANES), 1)
                for i, qcol in enumerate(qcols):
                    r0 = 2 * i * Q_BLK
                    o_ref[0, pl.ds(row, Q_BLK), qcol:qcol + PAIR] = jnp.where(
                        lo, out[r0:r0 + Q_BLK], out[r0 + Q_BLK:r0 + 2 * Q_BLK]).astype(BF16)
                    if want_lse:
                        for half in range(2):
                            head = 2 * (grp * stack + i) + half
                            rows_h = lse_rows[r0 + half * Q_BLK:r0 + (half + 1) * Q_BLK]
                            lse_tile = jnp.where(lane_l == head, rows_h, lse_tile)
            if want_lse:
                lse_ref[0, pl.ds(row, Q_BLK), c * LANES:(c + 1) * LANES] = lse_tile
        return carry

    lax.fori_loop(0, tq // Q_BLK, block, 0)


def _band_attn(q, k, v, bias, sink, *, seq_len, tq, half_window, n_cls, want_lse):
    n_rows, _, q_cols = q.shape
    kv_cols = k.shape[2]
    heads = bias.shape[1]
    q_pairs = heads // 2
    n_col = q_cols // (q_pairs * PAIR)
    kv_pairs = kv_cols // (n_col * PAIR)
    stack = q_pairs // kv_pairs
    tk = Q_BLK + 2 * half_window
    assert bias.shape == (3, heads, Q_BLK, tk)
    assert seq_len % tq == 0 and tq % Q_BLK == 0 and n_col % n_cls == 0
    bias = bias.reshape(3, kv_pairs, 2 * stack * Q_BLK, tk)

    q_blk = n_cls * q_pairs * PAIR
    kv_blk = n_cls * kv_pairs * PAIR
    in_specs = [pl.BlockSpec((1, tq, q_blk), lambda r, c, t: (r, t, c)),
                pl.BlockSpec((1, seq_len, kv_blk), lambda r, c, t: (r, 0, c)),
                pl.BlockSpec((1, seq_len, kv_blk), lambda r, c, t: (r, 0, c)),
                pl.BlockSpec(bias.shape, lambda r, c, t: (0, 0, 0, 0))]
    args = [q, k, v, bias]
    if sink is not None:
        sink_rows = jnp.repeat(sink.astype(F32), Q_BLK)[:, None]
        in_specs.append(pl.BlockSpec(sink_rows.shape, lambda r, c, t: (0, 0)))
        args.append(sink_rows)
    out_shape = [jax.ShapeDtypeStruct(q.shape, BF16)]
    out_specs = [pl.BlockSpec((1, tq, q_blk), lambda r, c, t: (r, t, c))]
    if want_lse:
        out_shape.append(jax.ShapeDtypeStruct((n_rows, seq_len, n_col * LANES), F32))
        out_specs.append(pl.BlockSpec((1, tq, n_cls * LANES), lambda r, c, t: (r, t, c)))
    kern = functools.partial(_band_attn_kernel, seq_len=seq_len, tq=tq, half_window=half_window,
                             n_cls=n_cls, q_pairs=q_pairs, kv_pairs=kv_pairs, stack=stack,
                             has_sink=sink is not None, want_lse=want_lse)
    return pl.pallas_call(
        kern,
        grid=(n_rows, n_col // n_cls, seq_len // tq),
        in_specs=in_specs,
        out_specs=out_specs,
        out_shape=out_shape,
        compiler_params=pltpu.CompilerParams(vmem_limit_bytes=VMEM_LIMIT),
        name=f"band_attn_hw{half_window}_len{seq_len}",
    )(*args)


def _merge_out_kernel(x_ref, oa_ref, sga_ref, ob_refs, lse_refs, sgb_ref, gate_ref,
                      expand_ref, wa_ref, wb_ref, wo_ref, y_ref, scr_o_refs, scr_l_refs):
    ya = (oa_ref[...].astype(F32) * sga_ref[...].astype(F32)).astype(BF16)
    dils = [dil for _, dil in B_GROUPS]
    lses = [_load_by_class(r, dil, s, LANES) for r, dil, s in zip(lse_refs, dils, scr_l_refs)]
    m = functools.reduce(jnp.maximum, lses)
    es = [jnp.exp(l - m) for l in lses]
    inv = 1.0 / functools.reduce(lambda a, b: a + b, es)
    yb = None
    for e, o_ref, dil, scr in zip(es, ob_refs, dils, scr_o_refs):
        alpha = jnp.dot((e * inv).astype(BF16), expand_ref[...], preferred_element_type=F32)
        term = alpha * _load_by_class(o_ref, dil, scr, B_WIDTH)
        yb = term if yb is None else yb + term
    yb = (yb * sgb_ref[...].astype(F32)).astype(BF16)
    br_a = jnp.dot(ya, wa_ref[...], preferred_element_type=F32)
    br_b = jnp.dot(yb, wb_ref[...], preferred_element_type=F32)
    merged = (gate_ref[:, :D_MODEL].astype(F32) * br_a
              + gate_ref[:, D_MODEL:].astype(F32) * br_b).astype(BF16)
    y_ref[...] = x_ref[...] + jnp.dot(merged, wo_ref[...], preferred_element_type=F32)


def _merge_out(x2, oa, sga, obs, lses, sgb, gates, expand, wa, wb, wo, tm):
    m = x2.shape[0]

    def rows(width):
        return pl.BlockSpec((tm, width), lambda i: (i, 0))

    def whole(a):
        return pl.BlockSpec(a.shape, lambda i: (0,) * a.ndim)

    def by_class(width):
        return [pl.BlockSpec((tm // dil, dil * width), lambda i: (i, 0)) for _, dil in B_GROUPS]

    n_g = len(B_GROUPS)
    scratch = [[pltpu.VMEM((B_WIDTH // LANES, tm, LANES), F32)] * n_g,
               [pltpu.VMEM((1, tm, LANES), F32)] * n_g]
    return pl.pallas_call(
        _merge_out_kernel,
        grid=(m // tm,),
        in_specs=[rows(D_MODEL), rows(A_WIDTH), rows(A_WIDTH), by_class(B_WIDTH),
                  by_class(LANES), rows(B_WIDTH), rows(2 * D_MODEL),
                  whole(expand), whole(wa), whole(wb), whole(wo)],
        out_specs=rows(D_MODEL),
        out_shape=jax.ShapeDtypeStruct((m, D_MODEL), F32),
        scratch_shapes=scratch,
        compiler_params=pltpu.CompilerParams(vmem_limit_bytes=VMEM_LIMIT),
        name="merge_out",
    )(x2, oa, sga, obs, lses, sgb, gates, expand, wa, wb, wo)


def _t5_bucket(rel):
    half = N_BUCKETS // 2
    max_exact = half // 2
    ret = (rel > 0).astype(jnp.int32) * half
    n = jnp.abs(rel)
    nf = jnp.maximum(n, max_exact).astype(jnp.float32)
    large = max_exact + (jnp.log(nf / max_exact) / math.log(MAX_DISTANCE / max_exact)
                         * (half - max_exact)).astype(jnp.int32)
    large = jnp.minimum(large, half - 1)
    return ret + jnp.where(n < max_exact, n, large)


def _bias_tiles(table, half_window, stride):
    tk = Q_BLK + 2 * half_window
    qi = jnp.arange(Q_BLK)[:, None]
    kj = jnp.arange(tk)[None, :]
    lead = jnp.asarray([0, half_window, 2 * half_window])[:, None, None]
    rel = (kj - qi)[None] - lead
    bucket = _t5_bucket(rel * stride)[:, None]
    table = table.astype(F32)
    bias = jnp.zeros((3, table.shape[1], Q_BLK, tk), F32)
    for b in range(N_BUCKETS):
        bias = jnp.where(bucket == b, table[b][None, :, None, None], bias)
    return jnp.where((jnp.abs(rel) <= half_window)[:, None], bias, NEG_INF)


_A_HEAD_ORDER = tuple(h for p in range(A_Q_HEADS // 2) for h in (p, p + A_Q_HEADS // 2))


def kernel(x, norm_gain, w_in, q_norm_a, k_norm_a, q_norm_b, k_norm_b, sink_a, rel_bias,
           w_branch_a, w_branch_b, b_merge, w_out):
    bsz, seq, d = x.shape
    n_g = len(B_GROUPS)
    scale = HEAD_DIM ** -0.5

    bd = jnp.asarray(np.kron(np.eye(256 // HEAD_DIM), np.full((HEAD_DIM, HEAD_DIM), 1.0 / HEAD_DIM)), BF16)
    expand = jnp.asarray(np.kron(np.eye(LANES, B_HEADS), np.ones((1, HEAD_DIM))), BF16)
    for layer in range(norm_gain.shape[0]):
        c_ga = A_WIDTH + 2 * PAIR + 3 * n_g * B_WIDTH
        wl = w_in[layer]

        def pair_order(c0):
            return [wl[:, c0 + h * HEAD_DIM:c0 + (h + 1) * HEAD_DIM] for h in _A_HEAD_ORDER]

        w = jnp.concatenate(pair_order(0) + [wl[:, A_WIDTH:c_ga]] + pair_order(c_ga)
                            + [wl[:, c_ga + A_WIDTH:]], axis=1).astype(BF16)
        wa = jnp.concatenate([w_branch_a[layer][h * HEAD_DIM:(h + 1) * HEAD_DIM] for h in _A_HEAD_ORDER],
                             axis=0).astype(BF16)
        wb = w_branch_b[layer].astype(BF16)
        wo = w_out[layer].astype(BF16)
        gqa = jnp.tile(q_norm_a[layer] * scale, A_WIDTH // HEAD_DIM)[None]
        gkva = jnp.tile(k_norm_a[layer], 2 * PAIR // HEAD_DIM)[None]
        gqb = jnp.tile(q_norm_b[layer] * scale, B_HEADS)[None]
        gkb = jnp.tile(k_norm_b[layer], B_HEADS)[None]
        bm = b_merge[layer].reshape(1, -1)
        sink = sink_a[layer][np.asarray(_A_HEAD_ORDER)]

        x2 = x.reshape(bsz * seq, d)
        qa, ka, va, qbs, kbs, vbs, sga, sgb, gates = _in_proj(
            x2, norm_gain[layer][None], w, bd, gqa, gkva, gqb, gkb, bm, tm=512)

        bias_a = _bias_tiles(rel_bias[:, :A_Q_HEADS][:, np.asarray(_A_HEAD_ORDER)], A_WINDOW, 1)
        (oa,) = _band_attn(qa.reshape(bsz, seq, A_WIDTH), ka.reshape(bsz, seq, PAIR),
                           va.reshape(bsz, seq, PAIR), bias_a, sink,
                           seq_len=seq, tq=1024, half_window=A_WINDOW, n_cls=1, want_lse=False)

        obs, lses = [], []
        for g, (window, dil) in enumerate(B_GROUPS):
            c0 = A_Q_HEADS + g * B_HEADS
            hw = window // (2 * dil)
            sub = seq // dil
            bias_g = _bias_tiles(rel_bias[:, c0:c0 + B_HEADS], hw, dil)
            view = (bsz, sub, dil * B_WIDTH)
            o_g, lse_g = _band_attn(qbs[g].reshape(view), kbs[g].reshape(view), vbs[g].reshape(view),
                                    bias_g, None, seq_len=sub, tq=min(sub, 1024), half_window=hw,
                                    n_cls=min(dil, 4), want_lse=True)
            obs.append(o_g.reshape(bsz * sub, dil * B_WIDTH))
            lses.append(lse_g.reshape(bsz * sub, dil * LANES))

        y2 = _merge_out(x2, oa.reshape(bsz * seq, A_WIDTH), sga, obs, lses, sgb, gates,
                        expand, wa, wb, wo, tm=512)
        x = y2.reshape(bsz, seq, d)
    return x
```

```python
import functools
import math

import jax
import jax.numpy as jnp
import numpy as np
from jax import lax
from jax.experimental import pallas as pl
from jax.experimental.pallas import tpu as pltpu

D_MODEL = 1024
HEAD_DIM = 64
A_Q_HEADS = 8
A_KV_HEADS = 2
A_WINDOW = 128
B_GROUPS = ((128, 1), (512, 4), (2048, 16))
B_HEADS = 8
A_WIDTH = A_Q_HEADS * HEAD_DIM
B_WIDTH = B_HEADS * HEAD_DIM
N_BUCKETS = 32
MAX_DISTANCE = 1024
EPS = 1e-6
NEG_INF = -1e30
LOG2E = 1.0 / math.log(2.0)

LANES = 128
PAIR = 2 * HEAD_DIM
Q_BLK = 128
VMEM_LIMIT = 56 * 1024 * 1024

BF16 = jnp.bfloat16
F32 = jnp.float32


def _head_norm(y, gain):
    lo = lax.broadcasted_iota(jnp.int32, (1, PAIR), 1) < HEAD_DIM
    parts = []
    for c in range(0, y.shape[1], PAIR):
        yc = y[:, c:c + PAIR]
        sq = yc * yc
        s_lo = jnp.sum(jnp.where(lo, sq, 0.0), axis=-1, keepdims=True)
        s_hi = jnp.sum(jnp.where(lo, 0.0, sq), axis=-1, keepdims=True)
        ms = jnp.where(lo, s_lo, s_hi) * (1.0 / HEAD_DIM)
        parts.append(yc * lax.rsqrt(ms + EPS))
    yn = parts[0] if len(parts) == 1 else jnp.concatenate(parts, axis=1)
    return yn * gain


def _silu(y):
    return y * (1.0 / (1.0 + jnp.exp(-y)))


def _store_by_class(y, dil, scr_ref, out_ref):
    rows, width = y.shape
    if dil == 1:
        out_ref[...] = y.astype(out_ref.dtype)
        return
    slabs = width // LANES
    for s in range(slabs):
        scr_ref[s] = y[:, s * LANES:(s + 1) * LANES]
    for c in range(dil):
        for s in range(slabs):
            col = c * width + s * LANES
            out_ref[:, col:col + LANES] = scr_ref[s, pl.ds(c, rows // dil, stride=dil), :].astype(out_ref.dtype)


def _load_by_class(ref, dil, scr_ref, width):
    if dil == 1:
        return ref[...].astype(F32)
    sub_rows = ref.shape[0]
    slabs = width // LANES
    for c in range(dil):
        for s in range(slabs):
            col = c * width + s * LANES
            scr_ref[s, pl.ds(c, sub_rows, stride=dil), :] = ref[:, col:col + LANES].astype(F32)
    parts = [scr_ref[s] for s in range(slabs)]
    return parts[0] if slabs == 1 else jnp.concatenate(parts, axis=1)


def _in_proj_kernel(x_ref, ng_ref, w_ref, gqa_ref, gka_ref, gqb_ref, gkb_ref, bm_ref,
                    qa_ref, ka_ref, va_ref, qb_refs, kb_refs, vb_refs, sga_ref, sgb_ref, gate_ref,
                    scr_refs):
    x = x_ref[...]
    ms = jnp.mean(x * x, axis=-1, keepdims=True)
    h = ((x * lax.rsqrt(ms + EPS)) * ng_ref[...]).astype(BF16)

    def proj(c0, width):
        return jnp.dot(h, w_ref[:, c0:c0 + width], preferred_element_type=F32)

    c = 0
    qa_ref[...] = _head_norm(proj(c, A_WIDTH), gqa_ref[...]).astype(BF16)
    c += A_WIDTH
    y = proj(c, 2 * PAIR)
    ka_ref[...] = _head_norm(y[:, :PAIR], gka_ref[...]).astype(BF16)
    va_ref[...] = y[:, PAIR:].astype(BF16)
    c += 2 * PAIR
    n_scr = 0
    for refs, gain_ref in ((qb_refs, gqb_ref), (kb_refs, gkb_ref), (vb_refs, None)):
        for g, (_, dil) in enumerate(B_GROUPS):
            y = proj(c, B_WIDTH)
            if gain_ref is not None:
                y = _head_norm(y, gain_ref[...])
            _store_by_class(y, dil, scr_refs[n_scr % len(scr_refs)], refs[g])
            n_scr += dil > 1
            c += B_WIDTH
    sga_ref[...] = _silu(proj(c, A_WIDTH)).astype(BF16)
    c += A_WIDTH
    sgb_ref[...] = _silu(proj(c, B_WIDTH)).astype(BF16)
    c += B_WIDTH
    for j in range(2 * D_MODEL // 512):
        y = proj(c, 512) + bm_ref[:, j * 512:(j + 1) * 512]
        gate_ref[:, j * 512:(j + 1) * 512] = (1.0 / (1.0 + jnp.exp(-y))).astype(BF16)
        c += 512


def _in_proj(x2, ng, w, gqa, gka, gqb, gkb, bm, tm):
    m = x2.shape[0]
    n_g = len(B_GROUPS)

    def rows(width):
        return pl.BlockSpec((tm, width), lambda i: (i, 0))

    def whole(a):
        return pl.BlockSpec(a.shape, lambda i: (0,) * a.ndim)

    def out(width):
        return jax.ShapeDtypeStruct((m, width), BF16)

    b_shape = [jax.ShapeDtypeStruct((m // dil, dil * B_WIDTH), BF16) for _, dil in B_GROUPS]
    b_spec = [pl.BlockSpec((tm // dil, dil * B_WIDTH), lambda i: (i, 0)) for _, dil in B_GROUPS]
    out_shape = (out(A_WIDTH), out(PAIR), out(PAIR), b_shape, b_shape, b_shape,
                 out(A_WIDTH), out(B_WIDTH), out(2 * D_MODEL))
    out_specs = (rows(A_WIDTH), rows(PAIR), rows(PAIR), b_spec, b_spec, b_spec,
                 rows(A_WIDTH), rows(B_WIDTH), rows(2 * D_MODEL))
    scratch = [[pltpu.VMEM((B_WIDTH // LANES, tm, LANES), F32)] * 2]
    return pl.pallas_call(
        _in_proj_kernel,
        grid=(m // tm,),
        in_specs=[rows(D_MODEL), whole(ng), pl.BlockSpec(memory_space=pltpu.VMEM),
                  whole(gqa), whole(gka), whole(gqb), whole(gkb), whole(bm)],
        out_specs=out_specs,
        out_shape=out_shape,
        scratch_shapes=scratch,
        compiler_params=pltpu.CompilerParams(vmem_limit_bytes=VMEM_LIMIT),
        name="in_proj",
    )(x2, ng, w, gqa, gka, gqb, gkb, bm)


def _band_attn_kernel(*refs, seq_len, tq, half_window, n_cls, q_pairs, kv_pairs, stack, unroll,
                      has_sink, want_lse):
    it = iter(refs)
    q_ref, k_ref, v_ref, bias_ref = next(it), next(it), next(it), next(it)
    sink_ref = next(it) if has_sink else None
    o_ref = next(it)
    lse_ref = next(it) if want_lse else None

    tk = Q_BLK + 2 * half_window
    n_blk = seq_len // Q_BLK
    lane = lax.broadcasted_iota(jnp.int32, (1, PAIR), 1)
    lo = lane < HEAD_DIM
    t = pl.program_id(2)

    def block(j, carry):
        row = pl.multiple_of(j * Q_BLK, Q_BLK)
        blk = t * (tq // Q_BLK) + j
        ws = pl.multiple_of(jnp.clip(blk * Q_BLK - half_window, 0, seq_len - tk), HEAD_DIM)
        sel = (blk > 0).astype(jnp.int32) + (blk == n_blk - 1).astype(jnp.int32)
        ones = jnp.ones((tk, PAIR), BF16)
        for c in range(n_cls):
            m_tile = jnp.zeros((Q_BLK, LANES), F32) if want_lse else None
            l_tile = jnp.ones((Q_BLK, LANES), F32) if want_lse else None
            for grp in range(q_pairs // stack):
                kcol = (c * kv_pairs + grp * stack * kv_pairs // q_pairs) * PAIR
                k = k_ref[0, pl.ds(ws, tk), kcol:kcol + PAIR]
                v = v_ref[0, pl.ds(ws, tk), kcol:kcol + PAIR]
                qcols = [(c * q_pairs + grp * stack + i) * PAIR for i in range(stack)]
                qs = []
                for qcol in qcols:
                    q = q_ref[0, pl.ds(row, Q_BLK), qcol:qcol + PAIR]
                    zero = jnp.zeros_like(q)
                    qs += [jnp.where(lo, q, zero), jnp.where(lo, zero, q)]
                s = lax.dot_general(jnp.concatenate(qs, axis=0), k, (((1,), (1,)), ((), ())),
                                    preferred_element_type=F32)
                s = s + bias_ref[sel, grp]
                heads = range(2 * stack * grp, 2 * stack * (grp + 1))
                ms, es = [], []
                for i, head in enumerate(heads):
                    s_h = s[i * Q_BLK:(i + 1) * Q_BLK]
                    m_h = jnp.max(s_h, axis=-1, keepdims=True)
                    if has_sink:
                        m_h = jnp.maximum(m_h, sink_ref[head])
                    ms.append(m_h)
                    es.append(jnp.exp2(s_h - m_h).astype(BF16))
                pv = jnp.dot(jnp.concatenate(es, axis=0), jnp.concatenate([v, ones], axis=1),
                             preferred_element_type=F32)
                outs = []
                for i, head in enumerate(heads):
                    l_h = pv[i * Q_BLK:(i + 1) * Q_BLK, PAIR:]
                    if has_sink:
                        l_h = l_h + jnp.exp2(sink_ref[head] - ms[i])
                    outs.append(pv[i * Q_BLK:(i + 1) * Q_BLK, :PAIR] * (1.0 / l_h))
                    if want_lse:
                        lane_l = lax.broadcasted_iota(jnp.int32, (1, LANES), 1)
                        m_tile = jnp.where(lane_l == head, ms[i], m_tile)
                        l_tile = jnp.where(lane_l == head, l_h, l_tile)
                for i, qcol in enumerate(qcols):
                    o_ref[0, pl.ds(row, Q_BLK), qcol:qcol + PAIR] = jnp.where(
                        lo, outs[2 * i], outs[2 * i + 1]).astype(BF16)
            if want_lse:
                lse_ref[0, pl.ds(row, Q_BLK), c * LANES:(c + 1) * LANES] = m_tile + jnp.log2(l_tile)
        return carry

    lax.fori_loop(0, tq // Q_BLK, block, 0, unroll=unroll)


def _band_attn(q, k, v, bias, sink, *, seq_len, tq, half_window, n_cls, unroll, want_lse):
    n_rows, _, q_cols = q.shape
    kv_cols = k.shape[2]
    tk = Q_BLK + 2 * half_window
    stack = bias.shape[2] // (2 * Q_BLK)
    q_pairs = bias.shape[1] * stack
    n_col = q_cols // (q_pairs * PAIR)
    kv_pairs = kv_cols // (n_col * PAIR)
    assert bias.shape == (3, q_pairs // stack, 2 * stack * Q_BLK, tk) and q_pairs % kv_pairs == 0
    assert seq_len % tq == 0 and tq % Q_BLK == 0 and n_col % n_cls == 0

    q_blk = n_cls * q_pairs * PAIR
    kv_blk = n_cls * kv_pairs * PAIR
    in_specs = [pl.BlockSpec((1, tq, q_blk), lambda r, c, t: (r, t, c)),
                pl.BlockSpec((1, seq_len, kv_blk), lambda r, c, t: (r, 0, c)),
                pl.BlockSpec((1, seq_len, kv_blk), lambda r, c, t: (r, 0, c)),
                pl.BlockSpec(bias.shape, lambda r, c, t: (0, 0, 0, 0))]
    args = [q, k, v, bias]
    if sink is not None:
        in_specs.append(pl.BlockSpec(memory_space=pltpu.SMEM))
        args.append(sink.astype(F32))
    out_shape = [jax.ShapeDtypeStruct(q.shape, BF16)]
    out_specs = [pl.BlockSpec((1, tq, q_blk), lambda r, c, t: (r, t, c))]
    if want_lse:
        out_shape.append(jax.ShapeDtypeStruct((n_rows, seq_len, n_col * LANES), F32))
        out_specs.append(pl.BlockSpec((1, tq, n_cls * LANES), lambda r, c, t: (r, t, c)))
    kern = functools.partial(_band_attn_kernel, seq_len=seq_len, tq=tq, half_window=half_window,
                             n_cls=n_cls, q_pairs=q_pairs, kv_pairs=kv_pairs, stack=stack, unroll=unroll,
                             has_sink=sink is not None, want_lse=want_lse)
    return pl.pallas_call(
        kern,
        grid=(n_rows, n_col // n_cls, seq_len // tq),
        in_specs=in_specs,
        out_specs=out_specs,
        out_shape=out_shape,
        compiler_params=pltpu.CompilerParams(vmem_limit_bytes=VMEM_LIMIT),
        name=f"band_attn_hw{half_window}_len{seq_len}",
    )(*args)


def _merge_out_kernel(x_ref, oa_ref, sga_ref, ob_refs, lse_refs, sgb_ref, gate_ref,
                      expand_ref, wa_ref, wb_ref, wo_ref, y_ref, scr_o_refs, scr_l_refs):
    ya = (oa_ref[...].astype(F32) * sga_ref[...].astype(F32)).astype(BF16)
    dils = [dil for _, dil in B_GROUPS]
    lses = [_load_by_class(r, dil, s, LANES) for r, dil, s in zip(lse_refs, dils, scr_l_refs)]
    m = functools.reduce(jnp.maximum, lses)
    es = [jnp.exp2(l - m) for l in lses]
    inv = 1.0 / functools.reduce(lambda a, b: a + b, es)
    yb = None
    for e, o_ref, dil, scr in zip(es, ob_refs, dils, scr_o_refs):
        alpha = jnp.dot((e * inv).astype(BF16), expand_ref[...], preferred_element_type=F32)
        term = alpha * _load_by_class(o_ref, dil, scr, B_WIDTH)
        yb = term if yb is None else yb + term
    yb = (yb * sgb_ref[...].astype(F32)).astype(BF16)
    br_a = jnp.dot(ya, wa_ref[...], preferred_element_type=F32)
    br_b = jnp.dot(yb, wb_ref[...], preferred_element_type=F32)
    merged = (gate_ref[:, :D_MODEL].astype(F32) * br_a
              + gate_ref[:, D_MODEL:].astype(F32) * br_b).astype(BF16)
    y_ref[...] = x_ref[...] + jnp.dot(merged, wo_ref[...], preferred_element_type=F32)


def _merge_out(x2, oa, sga, obs, lses, sgb, gates, expand, wa, wb, wo, tm):
    m = x2.shape[0]

    def rows(width):
        return pl.BlockSpec((tm, width), lambda i: (i, 0))

    def whole(a):
        return pl.BlockSpec(a.shape, lambda i: (0,) * a.ndim)

    def by_class(width):
        return [pl.BlockSpec((tm // dil, dil * width), lambda i: (i, 0)) for _, dil in B_GROUPS]

    n_g = len(B_GROUPS)
    scratch = [[pltpu.VMEM((B_WIDTH // LANES, tm, LANES), F32)] * n_g,
               [pltpu.VMEM((1, tm, LANES), F32)] * n_g]
    return pl.pallas_call(
        _merge_out_kernel,
        grid=(m // tm,),
        in_specs=[rows(D_MODEL), rows(A_WIDTH), rows(A_WIDTH), by_class(B_WIDTH),
                  by_class(LANES), rows(B_WIDTH), rows(2 * D_MODEL),
                  whole(expand), whole(wa), whole(wb), whole(wo)],
        out_specs=rows(D_MODEL),
        out_shape=jax.ShapeDtypeStruct((m, D_MODEL), F32),
        scratch_shapes=scratch,
        compiler_params=pltpu.CompilerParams(vmem_limit_bytes=VMEM_LIMIT),
        name="merge_out",
    )(x2, oa, sga, obs, lses, sgb, gates, expand, wa, wb, wo)


def _t5_bucket(rel):
    half = N_BUCKETS // 2
    max_exact = half // 2
    ret = (rel > 0).astype(jnp.int32) * half
    n = jnp.abs(rel)
    nf = jnp.maximum(n, max_exact).astype(jnp.float32)
    large = max_exact + (jnp.log(nf / max_exact) / math.log(MAX_DISTANCE / max_exact)
                         * (half - max_exact)).astype(jnp.int32)
    large = jnp.minimum(large, half - 1)
    return ret + jnp.where(n < max_exact, n, large)


def _bias_tiles(table, half_window, stride, group_heads):
    tk = Q_BLK + 2 * half_window
    n_grp = table.shape[1] // group_heads
    qi = (jnp.arange(group_heads * Q_BLK) % Q_BLK)[:, None]
    kj = jnp.arange(tk)[None, :]
    lead = jnp.asarray([0, half_window, 2 * half_window])[:, None, None]
    rel = ((kj - qi)[None] - lead)[:, None]
    bucket = _t5_bucket(rel * stride)
    rows = jnp.repeat((table.astype(F32) * LOG2E).reshape(N_BUCKETS, n_grp, group_heads), Q_BLK, axis=2)
    bias = jnp.zeros((3, n_grp, group_heads * Q_BLK, tk), F32)
    for b in range(N_BUCKETS):
        bias = jnp.where(bucket == b, rows[b][None, :, :, None], bias)
    return jnp.where(jnp.abs(rel) <= half_window, bias, NEG_INF)


_A_HEAD_ORDER = tuple(h for p in range(A_Q_HEADS // 2) for h in (p, p + A_Q_HEADS // 2))


def kernel(x, norm_gain, w_in, q_norm_a, k_norm_a, q_norm_b, k_norm_b, sink_a, rel_bias,
           w_branch_a, w_branch_b, b_merge, w_out):
    bsz, seq, d = x.shape
    n_g = len(B_GROUPS)
    scale = HEAD_DIM ** -0.5

    expand = jnp.asarray(np.kron(np.eye(LANES, B_HEADS), np.ones((1, HEAD_DIM))), BF16)
    for layer in range(norm_gain.shape[0]):
        c_ga = A_WIDTH + 2 * PAIR + 3 * n_g * B_WIDTH
        wl = w_in[layer]

        def pair_order(c0):
            return [wl[:, c0 + h * HEAD_DIM:c0 + (h + 1) * HEAD_DIM] for h in _A_HEAD_ORDER]

        w = jnp.concatenate(pair_order(0) + [wl[:, A_WIDTH:c_ga]] + pair_order(c_ga)
                            + [wl[:, c_ga + A_WIDTH:]], axis=1).astype(BF16)
        wa = jnp.concatenate([w_branch_a[layer][h * HEAD_DIM:(h + 1) * HEAD_DIM] for h in _A_HEAD_ORDER],
                             axis=0).astype(BF16)
        wb = w_branch_b[layer].astype(BF16)
        wo = w_out[layer].astype(BF16)
        gqa = jnp.tile(q_norm_a[layer] * (scale * LOG2E), A_WIDTH // HEAD_DIM)[None]
        gka = jnp.tile(k_norm_a[layer], A_KV_HEADS)[None]
        gqb = jnp.tile(q_norm_b[layer] * (scale * LOG2E), B_HEADS)[None]
        gkb = jnp.tile(k_norm_b[layer], B_HEADS)[None]
        bm = b_merge[layer].reshape(1, -1)
        sink = sink_a[layer][np.asarray(_A_HEAD_ORDER)] * LOG2E

        x2 = x.reshape(bsz * seq, d)
        qa, ka, va, qbs, kbs, vbs, sga, sgb, gates = _in_proj(
            x2, norm_gain[layer][None], w, gqa, gka, gqb, gkb, bm, tm=512)

        bias_a = _bias_tiles(rel_bias[:, :A_Q_HEADS][:, np.asarray(_A_HEAD_ORDER)], A_WINDOW, 1, 8)
        (oa,) = _band_attn(qa.reshape(bsz, seq, A_WIDTH), ka.reshape(bsz, seq, PAIR),
                           va.reshape(bsz, seq, PAIR), bias_a, sink,
                           seq_len=seq, tq=1024, half_window=A_WINDOW, n_cls=1, unroll=8, want_lse=False)

        obs, lses = [], []
        for g, (window, dil) in enumerate(B_GROUPS):
            c0 = A_Q_HEADS + g * B_HEADS
            hw = window // (2 * dil)
            sub = seq // dil
            bias_g = _bias_tiles(rel_bias[:, c0:c0 + B_HEADS], hw, dil, 2)
            view = (bsz, sub, dil * B_WIDTH)
            o_g, lse_g = _band_attn(qbs[g].reshape(view), kbs[g].reshape(view), vbs[g].reshape(view),
                                    bias_g, None, seq_len=sub, tq=min(sub, 1024), half_window=hw,
                                    n_cls=min(dil, 4), unroll=max(1, 4 // dil), want_lse=True)
            obs.append(o_g.reshape(bsz * sub, dil * B_WIDTH))
            lses.append(lse_g.reshape(bsz * sub, dil * LANES))

        y2 = _merge_out(x2, oa.reshape(bsz * seq, A_WIDTH), sga, obs, lses, sgb, gates,
                        expand, wa, wb, wo, tm=512)
        x = y2.reshape(bsz, seq, d)
    return x
```

```python
import functools
import math

import jax
import jax.numpy as jnp
import numpy as np
from jax import lax
from jax.experimental import pallas as pl
from jax.experimental.pallas import tpu as pltpu

D_MODEL = 1024
HEAD_DIM = 64
A_Q_HEADS = 8
A_KV_HEADS = 2
A_WINDOW = 128
B_GROUPS = ((128, 1), (512, 4), (2048, 16))
B_HEADS = 8
A_WIDTH = A_Q_HEADS * HEAD_DIM
B_WIDTH = B_HEADS * HEAD_DIM
N_BUCKETS = 32
MAX_DISTANCE = 1024
EPS = 1e-6
NEG_INF = -1e30
LOG2E = 1.0 / math.log(2.0)

LANES = 128
PAIR = 2 * HEAD_DIM
Q_BLK = 128
SEG = 256
VMEM_LIMIT = 56 * 1024 * 1024

BF16 = jnp.bfloat16
F32 = jnp.float32


def _head_norm(y, gain):
    lo = lax.broadcasted_iota(jnp.int32, (1, PAIR), 1) < HEAD_DIM
    parts = []
    for c in range(0, y.shape[1], PAIR):
        yc = y[:, c:c + PAIR]
        sq = yc * yc
        s_lo = jnp.sum(jnp.where(lo, sq, 0.0), axis=-1, keepdims=True)
        s_hi = jnp.sum(jnp.where(lo, 0.0, sq), axis=-1, keepdims=True)
        ms = jnp.where(lo, s_lo, s_hi) * (1.0 / HEAD_DIM)
        parts.append(yc * lax.rsqrt(ms + EPS))
    yn = parts[0] if len(parts) == 1 else jnp.concatenate(parts, axis=1)
    return yn * gain


def _silu(y):
    return y * (1.0 / (1.0 + jnp.exp(-y)))


def _store_by_class(y, dil, scr_ref, out_ref, col0, width):
    rows, w = y.shape
    if dil == 1:
        out_ref[:, col0:col0 + w] = y.astype(out_ref.dtype)
        return
    slabs = w // LANES
    for s in range(slabs):
        scr_ref[s] = y[:, s * LANES:(s + 1) * LANES]
    for c in range(dil):
        for s in range(slabs):
            col = c * width + col0 + s * LANES
            out_ref[:, col:col + LANES] = scr_ref[s, pl.ds(c, rows // dil, stride=dil), :].astype(out_ref.dtype)


def _load_by_class(ref, dil, scr_ref, width):
    if dil == 1:
        return ref[...].astype(F32)
    sub_rows = ref.shape[0]
    slabs = width // LANES
    for c in range(dil):
        for s in range(slabs):
            col = c * width + s * LANES
            scr_ref[s, pl.ds(c, sub_rows, stride=dil), :] = ref[:, col:col + LANES].astype(F32)
    parts = [scr_ref[s] for s in range(slabs)]
    return parts[0] if slabs == 1 else jnp.concatenate(parts, axis=1)


def _in_proj_kernel(x_ref, ng_ref, w_ref, gqa_ref, gka_ref, gqb_ref, gkb_ref, bm_ref,
                    qa_ref, ka_ref, va_ref, qb_refs, kb_refs, vb_refs, sga_ref, sgb_ref, gate_ref,
                    scr_refs):
    x = x_ref[...]
    ms = jnp.mean(x * x, axis=-1, keepdims=True)
    h = ((x * lax.rsqrt(ms + EPS)) * ng_ref[...]).astype(BF16)

    def segments(c0, width):
        for off in range(0, width, SEG):
            yield off, jnp.dot(h, w_ref[:, c0 + off:c0 + off + SEG], preferred_element_type=F32)

    c = 0
    for off, y in segments(c, A_WIDTH):
        qa_ref[:, off:off + SEG] = _head_norm(y, gqa_ref[:, off:off + SEG]).astype(BF16)
    c += A_WIDTH
    for _, y in segments(c, 2 * PAIR):
        ka_ref[...] = _head_norm(y[:, :PAIR], gka_ref[...]).astype(BF16)
        va_ref[...] = y[:, PAIR:].astype(BF16)
    c += 2 * PAIR
    n_scr = 0
    for refs, gain_ref in ((qb_refs, gqb_ref), (kb_refs, gkb_ref), (vb_refs, None)):
        for g, (_, dil) in enumerate(B_GROUPS):
            for off, y in segments(c, B_WIDTH):
                if gain_ref is not None:
                    y = _head_norm(y, gain_ref[:, off:off + SEG])
                _store_by_class(y, dil, scr_refs[n_scr % len(scr_refs)], refs[g], off, B_WIDTH)
                n_scr += dil > 1
            c += B_WIDTH
    for out_ref, width in ((sga_ref, A_WIDTH), (sgb_ref, B_WIDTH)):
        for off, y in segments(c, width):
            out_ref[:, off:off + SEG] = _silu(y).astype(BF16)
        c += width
    for off, y in segments(c, 2 * D_MODEL):
        y = y + bm_ref[:, off:off + SEG]
        gate_ref[:, off:off + SEG] = (1.0 / (1.0 + jnp.exp(-y))).astype(BF16)


def _in_proj(x2, ng, w, gqa, gka, gqb, gkb, bm, tm):
    m = x2.shape[0]
    n_g = len(B_GROUPS)

    def rows(width):
        return pl.BlockSpec((tm, width), lambda i: (i, 0))

    def whole(a):
        return pl.BlockSpec(a.shape, lambda i: (0,) * a.ndim)

    def out(width):
        return jax.ShapeDtypeStruct((m, width), BF16)

    b_shape = [jax.ShapeDtypeStruct((m // dil, dil * B_WIDTH), BF16) for _, dil in B_GROUPS]
    b_spec = [pl.BlockSpec((tm // dil, dil * B_WIDTH), lambda i: (i, 0)) for _, dil in B_GROUPS]
    out_shape = (out(A_WIDTH), out(PAIR), out(PAIR), b_shape, b_shape, b_shape,
                 out(A_WIDTH), out(B_WIDTH), out(2 * D_MODEL))
    out_specs = (rows(A_WIDTH), rows(PAIR), rows(PAIR), b_spec, b_spec, b_spec,
                 rows(A_WIDTH), rows(B_WIDTH), rows(2 * D_MODEL))
    scratch = [[pltpu.VMEM((SEG // LANES, tm, LANES), F32)] * 2]
    return pl.pallas_call(
        _in_proj_kernel,
        grid=(m // tm,),
        in_specs=[rows(D_MODEL), whole(ng), pl.BlockSpec(memory_space=pltpu.VMEM),
                  whole(gqa), whole(gka), whole(gqb), whole(gkb), whole(bm)],
        out_specs=out_specs,
        out_shape=out_shape,
        scratch_shapes=scratch,
        compiler_params=pltpu.CompilerParams(vmem_limit_bytes=VMEM_LIMIT),
        name="in_proj",
    )(x2, ng, w, gqa, gka, gqb, gkb, bm)


def _band_attn_kernel(*refs, seq_len, tq, half_window, n_cls, q_pairs, kv_pairs, stack, unroll,
                      has_sink, want_lse):
    it = iter(refs)
    q_ref, k_ref, v_ref, bias_row_ref = next(it), next(it), next(it), next(it)
    sink_ref = next(it) if has_sink else None
    o_ref = next(it)
    lse_ref = next(it) if want_lse else None
    bias_ref = next(it)

    tk = Q_BLK + 2 * half_window
    n_blk = seq_len // Q_BLK
    lane = lax.broadcasted_iota(jnp.int32, (1, PAIR), 1)
    lo = lane < HEAD_DIM
    t = pl.program_id(2)

    @pl.when((pl.program_id(0) == 0) & (pl.program_id(1) == 0) & (t == 0))
    def _():
        width = bias_row_ref.shape[1]
        for li, lead in enumerate((0, half_window, 2 * half_window)):
            shift = width - (Q_BLK - 1 + 2 * half_window - lead)
            for head in range(2 * q_pairs):
                base = jnp.broadcast_to(bias_row_ref[head:head + 1, :], (Q_BLK, width))
                tile = pltpu.roll(base, shift, 1, stride=1, stride_axis=0)
                grp, idx = divmod(head, 2 * stack)
                bias_ref[li, grp, idx * Q_BLK:(idx + 1) * Q_BLK, :] = tile[:, :tk]

    def block(j, carry):
        row = pl.multiple_of(j * Q_BLK, Q_BLK)
        blk = t * (tq // Q_BLK) + j
        ws = pl.multiple_of(jnp.clip(blk * Q_BLK - half_window, 0, seq_len - tk), HEAD_DIM)
        sel = (blk > 0).astype(jnp.int32) + (blk == n_blk - 1).astype(jnp.int32)
        ones = jnp.ones((tk, PAIR), BF16)
        for c in range(n_cls):
            m_tile = jnp.zeros((Q_BLK, LANES), F32) if want_lse else None
            l_tile = jnp.ones((Q_BLK, LANES), F32) if want_lse else None
            for grp in range(q_pairs // stack):
                kcol = (c * kv_pairs + grp * stack * kv_pairs // q_pairs) * PAIR
                k = k_ref[0, pl.ds(ws, tk), kcol:kcol + PAIR]
                v = v_ref[0, pl.ds(ws, tk), kcol:kcol + PAIR]
                qcols = [(c * q_pairs + grp * stack + i) * PAIR for i in range(stack)]
                qs = []
                for qcol in qcols:
                    q = q_ref[0, pl.ds(row, Q_BLK), qcol:qcol + PAIR]
                    zero = jnp.zeros_like(q)
                    qs += [jnp.where(lo, q, zero), jnp.where(lo, zero, q)]
                s = lax.dot_general(jnp.concatenate(qs, axis=0), k, (((1,), (1,)), ((), ())),
                                    preferred_element_type=F32)
                s = s + bias_ref[sel, grp]
                heads = range(2 * stack * grp, 2 * stack * (grp + 1))
                ms, es = [], []
                for i, head in enumerate(heads):
                    s_h = s[i * Q_BLK:(i + 1) * Q_BLK]
                    m_h = jnp.max(s_h, axis=-1, keepdims=True)
                    if has_sink:
                        m_h = jnp.maximum(m_h, sink_ref[head])
                    ms.append(m_h)
                    es.append(jnp.exp2(s_h - m_h).astype(BF16))
                pv = jnp.dot(jnp.concatenate(es, axis=0), jnp.concatenate([v, ones], axis=1),
                             preferred_element_type=F32)
                outs = []
                for i, head in enumerate(heads):
                    l_h = pv[i * Q_BLK:(i + 1) * Q_BLK, PAIR:]
                    if has_sink:
                        l_h = l_h + jnp.exp2(sink_ref[head] - ms[i])
                    if want_lse:
                        outs.append(pv[i * Q_BLK:(i + 1) * Q_BLK, :PAIR])
                        lane_l = lax.broadcasted_iota(jnp.int32, (1, LANES), 1)
                        m_tile = jnp.where(lane_l == head, ms[i], m_tile)
                        l_tile = jnp.where(lane_l == head, l_h, l_tile)
                    else:
                        outs.append(pv[i * Q_BLK:(i + 1) * Q_BLK, :PAIR] * (1.0 / l_h))
                for i, qcol in enumerate(qcols):
                    o_ref[0, pl.ds(row, Q_BLK), qcol:qcol + PAIR] = jnp.where(
                        lo, outs[2 * i], outs[2 * i + 1]).astype(BF16)
            if want_lse:
                lse_ref[0, pl.ds(row, Q_BLK), 2 * c * LANES:(2 * c + 1) * LANES] = m_tile
                lse_ref[0, pl.ds(row, Q_BLK), (2 * c + 1) * LANES:(2 * c + 2) * LANES] = l_tile
        return carry

    lax.fori_loop(0, tq // Q_BLK, block, 0, unroll=unroll)


def _band_attn(q, k, v, bias_rows, sink, *, seq_len, tq, half_window, stack, n_cls, unroll, want_lse):
    n_rows, _, q_cols = q.shape
    kv_cols = k.shape[2]
    tk = Q_BLK + 2 * half_window
    q_pairs = bias_rows.shape[0] // 2
    n_col = q_cols // (q_pairs * PAIR)
    kv_pairs = kv_cols // (n_col * PAIR)
    assert q_pairs % stack == 0 and q_pairs % kv_pairs == 0 and bias_rows.shape[1] % LANES == 0
    assert seq_len % tq == 0 and tq % Q_BLK == 0 and n_col % n_cls == 0

    q_blk = n_cls * q_pairs * PAIR
    kv_blk = n_cls * kv_pairs * PAIR
    in_specs = [pl.BlockSpec((1, tq, q_blk), lambda r, c, t: (r, t, c)),
                pl.BlockSpec((1, seq_len, kv_blk), lambda r, c, t: (r, 0, c)),
                pl.BlockSpec((1, seq_len, kv_blk), lambda r, c, t: (r, 0, c)),
                pl.BlockSpec(bias_rows.shape, lambda r, c, t: (0, 0))]
    args = [q, k, v, bias_rows]
    if sink is not None:
        in_specs.append(pl.BlockSpec(memory_space=pltpu.SMEM))
        args.append(sink.astype(F32))
    out_shape = [jax.ShapeDtypeStruct(q.shape, BF16)]
    out_specs = [pl.BlockSpec((1, tq, q_blk), lambda r, c, t: (r, t, c))]
    if want_lse:
        out_shape.append(jax.ShapeDtypeStruct((n_rows, seq_len, n_col * 2 * LANES), F32))
        out_specs.append(pl.BlockSpec((1, tq, n_cls * 2 * LANES), lambda r, c, t: (r, t, c)))
    kern = functools.partial(_band_attn_kernel, seq_len=seq_len, tq=tq, half_window=half_window,
                             n_cls=n_cls, q_pairs=q_pairs, kv_pairs=kv_pairs, stack=stack, unroll=unroll,
                             has_sink=sink is not None, want_lse=want_lse)
    return pl.pallas_call(
        kern,
        grid=(n_rows, n_col // n_cls, seq_len // tq),
        in_specs=in_specs,
        out_specs=out_specs,
        out_shape=out_shape,
        scratch_shapes=[pltpu.VMEM((3, q_pairs // stack, 2 * stack * Q_BLK, tk), F32)],
        compiler_params=pltpu.CompilerParams(vmem_limit_bytes=VMEM_LIMIT,
                                             dimension_semantics=("arbitrary",) * 3),
        name=f"band_attn_hw{half_window}_len{seq_len}",
    )(*args)


def _merge_out_kernel(x_ref, oa_ref, sga_ref, ob_refs, lse_refs, sgb_ref, gate_ref,
                      expand_ref, wa_ref, wb_ref, wo_ref, y_ref, scr_o_refs, scr_l_refs):
    ya = (oa_ref[...].astype(F32) * sga_ref[...].astype(F32)).astype(BF16)
    dils = [dil for _, dil in B_GROUPS]
    stats = [_load_by_class(r, dil, s, 2 * LANES) for r, dil, s in zip(lse_refs, dils, scr_l_refs)]
    m = functools.reduce(jnp.maximum, [st[:, :LANES] for st in stats])
    es = [jnp.exp2(st[:, :LANES] - m) for st in stats]
    inv = 1.0 / functools.reduce(lambda a, b: a + b, [e * st[:, LANES:] for e, st in zip(es, stats)])
    yb = None
    for e, o_ref, dil, scr in zip(es, ob_refs, dils, scr_o_refs):
        alpha = jnp.dot((e * inv).astype(BF16), expand_ref[...], preferred_element_type=F32)
        term = alpha * _load_by_class(o_ref, dil, scr, B_WIDTH)
        yb = term if yb is None else yb + term
    yb = (yb * sgb_ref[...].astype(F32)).astype(BF16)
    br_a = jnp.dot(ya, wa_ref[...], preferred_element_type=F32)
    br_b = jnp.dot(yb, wb_ref[...], preferred_element_type=F32)
    merged = (gate_ref[:, :D_MODEL].astype(F32) * br_a
              + gate_ref[:, D_MODEL:].astype(F32) * br_b).astype(BF16)
    y_ref[...] = x_ref[...] + jnp.dot(merged, wo_ref[...], preferred_element_type=F32)


def _merge_out(x2, oa, sga, obs, lses, sgb, gates, expand, wa, wb, wo, tm):
    m = x2.shape[0]

    def rows(width):
        return pl.BlockSpec((tm, width), lambda i: (i, 0))

    def whole(a):
        return pl.BlockSpec(a.shape, lambda i: (0,) * a.ndim)

    def by_class(width):
        return [pl.BlockSpec((tm // dil, dil * width), lambda i: (i, 0)) for _, dil in B_GROUPS]

    n_g = len(B_GROUPS)
    scratch = [[pltpu.VMEM((B_WIDTH // LANES, tm, LANES), F32)] * n_g,
               [pltpu.VMEM((2, tm, LANES), F32)] * n_g]
    return pl.pallas_call(
        _merge_out_kernel,
        grid=(m // tm,),
        in_specs=[rows(D_MODEL), rows(A_WIDTH), rows(A_WIDTH), by_class(B_WIDTH),
                  by_class(2 * LANES), rows(B_WIDTH), rows(2 * D_MODEL),
                  whole(expand), whole(wa), whole(wb), whole(wo)],
        out_specs=rows(D_MODEL),
        out_shape=jax.ShapeDtypeStruct((m, D_MODEL), F32),
        scratch_shapes=scratch,
        compiler_params=pltpu.CompilerParams(vmem_limit_bytes=VMEM_LIMIT),
        name="merge_out",
    )(x2, oa, sga, obs, lses, sgb, gates, expand, wa, wb, wo)


def _t5_bucket(rel):
    half = N_BUCKETS // 2
    max_exact = half // 2
    ret = (rel > 0).astype(jnp.int32) * half
    n = jnp.abs(rel)
    nf = jnp.maximum(n, max_exact).astype(jnp.float32)
    large = max_exact + jnp.floor(jnp.log(nf / max_exact) / math.log(MAX_DISTANCE / max_exact)
                                  * (half - max_exact)).astype(jnp.int32)
    large = jnp.minimum(large, half - 1)
    return ret + jnp.where(n < max_exact, n, large)


def _bias_rows(table, half_window, stride):
    n_rel = 2 * Q_BLK - 1 + 4 * half_window
    width = -(-n_rel // LANES) * LANES
    rel = jnp.arange(width) - (Q_BLK - 1 + 2 * half_window)
    bucket = _t5_bucket(rel * stride)[None, :]
    tab = table.astype(F32) * LOG2E
    rows = jnp.zeros((table.shape[1], width), F32)
    for b in range(N_BUCKETS):
        rows = jnp.where(bucket == b, tab[b][:, None], rows)
    return jnp.where((jnp.abs(rel) <= half_window)[None, :], rows, NEG_INF)


_A_HEAD_ORDER = tuple(h for p in range(A_Q_HEADS // 2) for h in (p, p + A_Q_HEADS // 2))


def kernel(x, norm_gain, w_in, q_norm_a, k_norm_a, q_norm_b, k_norm_b, sink_a, rel_bias,
           w_branch_a, w_branch_b, b_merge, w_out):
    bsz, seq, d = x.shape
    n_g = len(B_GROUPS)
    scale = HEAD_DIM ** -0.5

    expand = jnp.asarray(np.kron(np.eye(LANES, B_HEADS), np.ones((1, HEAD_DIM))), BF16)
    for layer in range(norm_gain.shape[0]):
        c_ga = A_WIDTH + 2 * PAIR + 3 * n_g * B_WIDTH
        wl = w_in[layer]

        def pair_order(c0):
            return [wl[:, c0 + h * HEAD_DIM:c0 + (h + 1) * HEAD_DIM] for h in _A_HEAD_ORDER]

        w = jnp.concatenate(pair_order(0) + [wl[:, A_WIDTH:c_ga]] + pair_order(c_ga)
                            + [wl[:, c_ga + A_WIDTH:]], axis=1).astype(BF16)
        wa = jnp.concatenate([w_branch_a[layer][h * HEAD_DIM:(h + 1) * HEAD_DIM] for h in _A_HEAD_ORDER],
                             axis=0).astype(BF16)
        wb = w_branch_b[layer].astype(BF16)
        wo = w_out[layer].astype(BF16)
        gqa = jnp.tile(q_norm_a[layer] * (scale * LOG2E), A_WIDTH // HEAD_DIM)[None]
        gka = jnp.tile(k_norm_a[layer], A_KV_HEADS)[None]
        gqb = jnp.tile(q_norm_b[layer] * (scale * LOG2E), B_HEADS)[None]
        gkb = jnp.tile(k_norm_b[layer], B_HEADS)[None]
        bm = b_merge[layer].reshape(1, -1)
        sink = sink_a[layer][np.asarray(_A_HEAD_ORDER)] * LOG2E

        x2 = x.reshape(bsz * seq, d)
        qa, ka, va, qbs, kbs, vbs, sga, sgb, gates = _in_proj(
            x2, norm_gain[layer][None], w, gqa, gka, gqb, gkb, bm, tm=512)

        bias_a = _bias_rows(rel_bias[:, :A_Q_HEADS][:, np.asarray(_A_HEAD_ORDER)], A_WINDOW, 1)
        (oa,) = _band_attn(qa.reshape(bsz, seq, A_WIDTH), ka.reshape(bsz, seq, PAIR),
                           va.reshape(bsz, seq, PAIR), bias_a, sink,
                           seq_len=seq, tq=1024, half_window=A_WINDOW, stack=A_Q_HEADS // 2, n_cls=1,
                           unroll=8, want_lse=False)

        obs, lses = [], []
        for g, (window, dil) in enumerate(B_GROUPS):
            c0 = A_Q_HEADS + g * B_HEADS
            hw = window // (2 * dil)
            sub = seq // dil
            bias_g = _bias_rows(rel_bias[:, c0:c0 + B_HEADS], hw, dil)
            view = (bsz, sub, dil * B_WIDTH)
            o_g, lse_g = _band_attn(qbs[g].reshape(view), kbs[g].reshape(view), vbs[g].reshape(view),
                                    bias_g, None, seq_len=sub, tq=min(sub, 1024), half_window=hw, stack=1,
                                    n_cls=min(dil, 4), unroll=max(1, 4 // dil), want_lse=True)
            obs.append(o_g.reshape(bsz * sub, dil * B_WIDTH))
            lses.append(lse_g.reshape(bsz * sub, dil * 2 * LANES))

        y2 = _merge_out(x2, oa.reshape(bsz * seq, A_WIDTH), sga, obs, lses, sgb, gates,
                        expand, wa, wb, wo, tm=512)
        x = y2.reshape(bsz, seq, d)
    return x
```

```python
import functools
import math

import jax
import jax.numpy as jnp
import numpy as np
from jax import lax
from jax.experimental import pallas as pl
from jax.experimental.pallas import tpu as pltpu

D_MODEL = 1024
HEAD_DIM = 64
A_Q_HEADS = 8
A_KV_HEADS = 2
A_WINDOW = 128
B_GROUPS = ((128, 1), (512, 4), (2048, 16))
B_HEADS = 8
A_WIDTH = A_Q_HEADS * HEAD_DIM
B_WIDTH = B_HEADS * HEAD_DIM
N_BUCKETS = 32
MAX_DISTANCE = 1024
EPS = 1e-6
NEG_INF = -1e30
LOG2E = 1.0 / math.log(2.0)

LANES = 128
PAIR = 2 * HEAD_DIM
Q_BLK = 128
SEG = 256
VMEM_LIMIT = 56 * 1024 * 1024

BF16 = jnp.bfloat16
F32 = jnp.float32


def _head_norm(y, gain):
    lo = lax.broadcasted_iota(jnp.int32, (1, PAIR), 1) < HEAD_DIM
    parts = []
    for c in range(0, y.shape[1], PAIR):
        yc = y[:, c:c + PAIR]
        sq = yc * yc
        s_lo = jnp.sum(jnp.where(lo, sq, 0.0), axis=-1, keepdims=True)
        s_hi = jnp.sum(jnp.where(lo, 0.0, sq), axis=-1, keepdims=True)
        ms = jnp.where(lo, s_lo, s_hi) * (1.0 / HEAD_DIM)
        parts.append(yc * lax.rsqrt(ms + EPS))
    yn = parts[0] if len(parts) == 1 else jnp.concatenate(parts, axis=1)
    return yn * gain


def _silu(y):
    return y * (1.0 / (1.0 + jnp.exp(-y)))


def _pair_order(y):
    lo = lax.broadcasted_iota(jnp.int32, (1, PAIR), 1) < HEAD_DIM
    n_pairs = A_Q_HEADS // 2
    cols = [y[:, g * PAIR:(g + 1) * PAIR] for g in range(n_pairs)]
    swapped = [pltpu.roll(col, HEAD_DIM, 1) for col in cols]
    out = []
    for p in range(n_pairs):
        first, second = p, p + n_pairs
        lo_src = cols[first // 2] if first % 2 == 0 else swapped[first // 2]
        hi_src = cols[second // 2] if second % 2 == 1 else swapped[second // 2]
        out.append(jnp.where(lo, lo_src, hi_src))
    return jnp.concatenate(out, axis=1)


def _store_by_class(y, dil, scr_ref, out_ref, col0, width):
    rows, w = y.shape
    if dil == 1:
        out_ref[:, col0:col0 + w] = y.astype(out_ref.dtype)
        return
    slabs = w // LANES
    for s in range(slabs):
        scr_ref[s] = y[:, s * LANES:(s + 1) * LANES]
    for c in range(dil):
        for s in range(slabs):
            col = c * width + col0 + s * LANES
            out_ref[:, col:col + LANES] = scr_ref[s, pl.ds(c, rows // dil, stride=dil), :].astype(out_ref.dtype)


def _load_by_class(ref, dil, scr_ref, width):
    if dil == 1:
        return ref[...].astype(F32)
    sub_rows = ref.shape[0]
    slabs = width // LANES
    for c in range(dil):
        for s in range(slabs):
            col = c * width + s * LANES
            scr_ref[s, pl.ds(c, sub_rows, stride=dil), :] = ref[:, col:col + LANES].astype(F32)
    parts = [scr_ref[s] for s in range(slabs)]
    return parts[0] if slabs == 1 else jnp.concatenate(parts, axis=1)


def _in_proj_kernel(x_ref, ng_ref, w_ref, gqa_ref, gka_ref, gqb_ref, gkb_ref, bm_ref,
                    qa_ref, ka_ref, va_ref, qb_refs, kb_refs, vb_refs, sga_ref, sgb_ref, gate_ref,
                    scr_refs):
    x = x_ref[...]
    ms = jnp.mean(x * x, axis=-1, keepdims=True)
    h = ((x * lax.rsqrt(ms + EPS)) * ng_ref[...]).astype(BF16)

    def segments(c0, width):
        for off in range(0, width, SEG):
            yield off, jnp.dot(h, w_ref[:, c0 + off:c0 + off + SEG], preferred_element_type=F32)

    c = 0
    qa = [_head_norm(y, gqa_ref[:, off:off + SEG]) for off, y in segments(c, A_WIDTH)]
    qa_ref[...] = _pair_order(jnp.concatenate(qa, axis=1)).astype(BF16)
    c += A_WIDTH
    for _, y in segments(c, 2 * PAIR):
        ka_ref[...] = _head_norm(y[:, :PAIR], gka_ref[...]).astype(BF16)
        va_ref[...] = y[:, PAIR:].astype(BF16)
    c += 2 * PAIR
    n_scr = 0
    for refs, gain_ref in ((qb_refs, gqb_ref), (kb_refs, gkb_ref), (vb_refs, None)):
        for g, (_, dil) in enumerate(B_GROUPS):
            for off, y in segments(c, B_WIDTH):
                if gain_ref is not None:
                    y = _head_norm(y, gain_ref[:, off:off + SEG])
                _store_by_class(y, dil, scr_refs[n_scr % len(scr_refs)], refs[g], off, B_WIDTH)
                n_scr += dil > 1
            c += B_WIDTH
    sga = [_silu(y) for _, y in segments(c, A_WIDTH)]
    sga_ref[...] = _pair_order(jnp.concatenate(sga, axis=1)).astype(BF16)
    c += A_WIDTH
    for off, y in segments(c, B_WIDTH):
        sgb_ref[:, off:off + SEG] = _silu(y).astype(BF16)
    c += B_WIDTH
    for off, y in segments(c, 2 * D_MODEL):
        y = y + bm_ref[:, off:off + SEG]
        gate_ref[:, off:off + SEG] = (1.0 / (1.0 + jnp.exp(-y))).astype(BF16)


def _in_proj(x2, ng, w, gqa, gka, gqb, gkb, bm, tm):
    m = x2.shape[0]
    n_g = len(B_GROUPS)

    def rows(width):
        return pl.BlockSpec((tm, width), lambda i: (i, 0))

    def whole(a):
        return pl.BlockSpec(a.shape, lambda i: (0,) * a.ndim)

    def out(width):
        return jax.ShapeDtypeStruct((m, width), BF16)

    b_shape = [jax.ShapeDtypeStruct((m // dil, dil * B_WIDTH), BF16) for _, dil in B_GROUPS]
    b_spec = [pl.BlockSpec((tm // dil, dil * B_WIDTH), lambda i: (i, 0)) for _, dil in B_GROUPS]
    out_shape = (out(A_WIDTH), out(PAIR), out(PAIR), b_shape, b_shape, b_shape,
                 out(A_WIDTH), out(B_WIDTH), out(2 * D_MODEL))
    out_specs = (rows(A_WIDTH), rows(PAIR), rows(PAIR), b_spec, b_spec, b_spec,
                 rows(A_WIDTH), rows(B_WIDTH), rows(2 * D_MODEL))
    scratch = [[pltpu.VMEM((SEG // LANES, tm, LANES), F32)] * 2]
    return pl.pallas_call(
        _in_proj_kernel,
        grid=(m // tm,),
        in_specs=[rows(D_MODEL), whole(ng), pl.BlockSpec(memory_space=pltpu.VMEM),
                  whole(gqa), whole(gka), whole(gqb), whole(gkb), whole(bm)],
        out_specs=out_specs,
        out_shape=out_shape,
        scratch_shapes=scratch,
        compiler_params=pltpu.CompilerParams(vmem_limit_bytes=VMEM_LIMIT),
        name="in_proj",
    )(x2, ng, w, gqa, gka, gqb, gkb, bm)


def _band_attn_kernel(*refs, seq_len, tq, half_window, n_cls, q_pairs, kv_pairs, stack, unroll,
                      has_sink, want_lse):
    it = iter(refs)
    q_ref, k_ref, v_ref, bias_row_ref = next(it), next(it), next(it), next(it)
    sink_ref = next(it) if has_sink else None
    o_ref = next(it)
    lse_ref = next(it) if want_lse else None
    bias_ref = next(it)

    tk = Q_BLK + 2 * half_window
    n_blk = seq_len // Q_BLK
    lane = lax.broadcasted_iota(jnp.int32, (1, PAIR), 1)
    lo = lane < HEAD_DIM
    t = pl.program_id(2)

    @pl.when((pl.program_id(0) == 0) & (pl.program_id(1) == 0) & (t == 0))
    def _():
        width = bias_row_ref.shape[1]
        for li, lead in enumerate((0, half_window, 2 * half_window)):
            shift = width - (Q_BLK - 1 + 2 * half_window - lead)
            for head in range(2 * q_pairs):
                base = jnp.broadcast_to(bias_row_ref[head:head + 1, :], (Q_BLK, width))
                tile = pltpu.roll(base, shift, 1, stride=1, stride_axis=0)
                grp, idx = divmod(head, 2 * stack)
                bias_ref[li, grp, idx * Q_BLK:(idx + 1) * Q_BLK, :] = tile[:, :tk]

    def block(j, carry):
        row = pl.multiple_of(j * Q_BLK, Q_BLK)
        blk = t * (tq // Q_BLK) + j
        ws = pl.multiple_of(jnp.clip(blk * Q_BLK - half_window, 0, seq_len - tk), HEAD_DIM)
        sel = (blk > 0).astype(jnp.int32) + (blk == n_blk - 1).astype(jnp.int32)
        ones = jnp.ones((tk, PAIR), BF16)
        for c in range(n_cls):
            lane_l = lax.broadcasted_iota(jnp.int32, (1, LANES), 1)
            st_tile = jnp.where(lane_l < HEAD_DIM, jnp.zeros((Q_BLK, LANES), F32), 1.0) if want_lse else None
            for grp in range(q_pairs // stack):
                kcol = (c * kv_pairs + grp * stack * kv_pairs // q_pairs) * PAIR
                k = k_ref[0, pl.ds(ws, tk), kcol:kcol + PAIR]
                v = v_ref[0, pl.ds(ws, tk), kcol:kcol + PAIR]
                qcols = [(c * q_pairs + grp * stack + i) * PAIR for i in range(stack)]
                qs = []
                for qcol in qcols:
                    q = q_ref[0, pl.ds(row, Q_BLK), qcol:qcol + PAIR]
                    zero = jnp.zeros_like(q)
                    qs += [jnp.where(lo, q, zero), jnp.where(lo, zero, q)]
                s = lax.dot_general(jnp.concatenate(qs, axis=0), k, (((1,), (1,)), ((), ())),
                                    preferred_element_type=F32)
                s = s + bias_ref[sel, grp]
                heads = range(2 * stack * grp, 2 * stack * (grp + 1))
                ms, es = [], []
                for i, head in enumerate(heads):
                    s_h = s[i * Q_BLK:(i + 1) * Q_BLK]
                    m_h = jnp.max(s_h, axis=-1, keepdims=True)
                    if has_sink:
                        m_h = jnp.maximum(m_h, sink_ref[head])
                    ms.append(m_h)
                    es.append(jnp.exp2(s_h - m_h).astype(BF16))
                pv = jnp.dot(jnp.concatenate(es, axis=0), jnp.concatenate([v, ones], axis=1),
                             preferred_element_type=F32)
                outs = []
                for i, head in enumerate(heads):
                    l_h = pv[i * Q_BLK:(i + 1) * Q_BLK, PAIR:]
                    if has_sink:
                        l_h = l_h + jnp.exp2(sink_ref[head] - ms[i])
                    if want_lse:
                        outs.append(pv[i * Q_BLK:(i + 1) * Q_BLK, :PAIR])
                        st_tile = jnp.where(lane_l == head, ms[i], st_tile)
                        st_tile = jnp.where(lane_l == HEAD_DIM + head, l_h, st_tile)
                    else:
                        outs.append(pv[i * Q_BLK:(i + 1) * Q_BLK, :PAIR] * (1.0 / l_h))
                for i, qcol in enumerate(qcols):
                    o_ref[0, pl.ds(row, Q_BLK), qcol:qcol + PAIR] = jnp.where(
                        lo, outs[2 * i], outs[2 * i + 1]).astype(BF16)
            if want_lse:
                lse_ref[0, pl.ds(row, Q_BLK), c * LANES:(c + 1) * LANES] = st_tile
        return carry

    lax.fori_loop(0, tq // Q_BLK, block, 0, unroll=unroll)


def _band_attn(q, k, v, bias_rows, sink, *, seq_len, tq, half_window, stack, n_cls, unroll, want_lse):
    n_rows, _, q_cols = q.shape
    kv_cols = k.shape[2]
    tk = Q_BLK + 2 * half_window
    q_pairs = bias_rows.shape[0] // 2
    n_col = q_cols // (q_pairs * PAIR)
    kv_pairs = kv_cols // (n_col * PAIR)
    assert q_pairs % stack == 0 and q_pairs % kv_pairs == 0 and bias_rows.shape[1] % LANES == 0
    assert seq_len % tq == 0 and tq % Q_BLK == 0 and n_col % n_cls == 0

    q_blk = n_cls * q_pairs * PAIR
    kv_blk = n_cls * kv_pairs * PAIR
    in_specs = [pl.BlockSpec((1, tq, q_blk), lambda r, c, t: (r, t, c)),
                pl.BlockSpec((1, seq_len, kv_blk), lambda r, c, t: (r, 0, c)),
                pl.BlockSpec((1, seq_len, kv_blk), lambda r, c, t: (r, 0, c)),
                pl.BlockSpec(bias_rows.shape, lambda r, c, t: (0, 0))]
    args = [q, k, v, bias_rows]
    if sink is not None:
        in_specs.append(pl.BlockSpec(memory_space=pltpu.SMEM))
        args.append(sink.astype(F32))
    out_shape = [jax.ShapeDtypeStruct(q.shape, BF16)]
    out_specs = [pl.BlockSpec((1, tq, q_blk), lambda r, c, t: (r, t, c))]
    if want_lse:
        out_shape.append(jax.ShapeDtypeStruct((n_rows, seq_len, n_col * LANES), F32))
        out_specs.append(pl.BlockSpec((1, tq, n_cls * LANES), lambda r, c, t: (r, t, c)))
    kern = functools.partial(_band_attn_kernel, seq_len=seq_len, tq=tq, half_window=half_window,
                             n_cls=n_cls, q_pairs=q_pairs, kv_pairs=kv_pairs, stack=stack, unroll=unroll,
                             has_sink=sink is not None, want_lse=want_lse)
    return pl.pallas_call(
        kern,
        grid=(n_rows, n_col // n_cls, seq_len // tq),
        in_specs=in_specs,
        out_specs=out_specs,
        out_shape=out_shape,
        scratch_shapes=[pltpu.VMEM((3, q_pairs // stack, 2 * stack * Q_BLK, tk), F32)],
        compiler_params=pltpu.CompilerParams(vmem_limit_bytes=VMEM_LIMIT,
                                             dimension_semantics=("arbitrary",) * 3),
        name=f"band_attn_hw{half_window}_len{seq_len}",
    )(*args)


def _merge_out_kernel(x_ref, oa_ref, sga_ref, ob_refs, lse_refs, sgb_ref, gate_ref,
                      expand_ref, wa_ref, wb_ref, wo_ref, y_ref, scr_o_refs, scr_l_refs):
    ya = (oa_ref[...].astype(F32) * sga_ref[...].astype(F32)).astype(BF16)
    dils = [dil for _, dil in B_GROUPS]
    stats = [_load_by_class(r, dil, s, LANES) for r, dil, s in zip(lse_refs, dils, scr_l_refs)]
    is_head = lax.broadcasted_iota(jnp.int32, (1, LANES), 1) < HEAD_DIM
    m = functools.reduce(jnp.maximum, stats)
    es = [jnp.exp2(st - m) for st in stats]
    den = functools.reduce(lambda a, b: a + b, [e * pltpu.roll(st, HEAD_DIM, 1) for e, st in zip(es, stats)])
    inv = jnp.where(is_head, 1.0 / den, 0.0)
    yb = None
    for e, o_ref, dil, scr in zip(es, ob_refs, dils, scr_o_refs):
        alpha = jnp.dot((e * inv).astype(BF16), expand_ref[...], preferred_element_type=F32)
        term = alpha * _load_by_class(o_ref, dil, scr, B_WIDTH)
        yb = term if yb is None else yb + term
    yb = (yb * sgb_ref[...].astype(F32)).astype(BF16)
    br_a = jnp.dot(ya, wa_ref[...], preferred_element_type=F32)
    br_b = jnp.dot(yb, wb_ref[...], preferred_element_type=F32)
    merged = (gate_ref[:, :D_MODEL].astype(F32) * br_a
              + gate_ref[:, D_MODEL:].astype(F32) * br_b).astype(BF16)
    y_ref[...] = x_ref[...] + jnp.dot(merged, wo_ref[...], preferred_element_type=F32)


def _merge_out(x2, oa, sga, obs, lses, sgb, gates, expand, wa, wb, wo, tm):
    m = x2.shape[0]

    def rows(width):
        return pl.BlockSpec((tm, width), lambda i: (i, 0))

    def whole(a):
        return pl.BlockSpec(a.shape, lambda i: (0,) * a.ndim)

    def by_class(width):
        return [pl.BlockSpec((tm // dil, dil * width), lambda i: (i, 0)) for _, dil in B_GROUPS]

    n_g = len(B_GROUPS)
    scratch = [[pltpu.VMEM((B_WIDTH // LANES, tm, LANES), F32)] * n_g,
               [pltpu.VMEM((1, tm, LANES), F32)] * n_g]
    return pl.pallas_call(
        _merge_out_kernel,
        grid=(m // tm,),
        in_specs=[rows(D_MODEL), rows(A_WIDTH), rows(A_WIDTH), by_class(B_WIDTH),
                  by_class(LANES), rows(B_WIDTH), rows(2 * D_MODEL),
                  whole(expand), whole(wa), whole(wb), whole(wo)],
        out_specs=rows(D_MODEL),
        out_shape=jax.ShapeDtypeStruct((m, D_MODEL), F32),
        scratch_shapes=scratch,
        compiler_params=pltpu.CompilerParams(vmem_limit_bytes=VMEM_LIMIT),
        name="merge_out",
    )(x2, oa, sga, obs, lses, sgb, gates, expand, wa, wb, wo)


def _t5_bucket(rel):
    half = N_BUCKETS // 2
    max_exact = half // 2
    ret = (rel > 0).astype(jnp.int32) * half
    n = jnp.abs(rel)
    nf = jnp.maximum(n, max_exact).astype(jnp.float32)
    large = max_exact + jnp.floor(jnp.log(nf / max_exact) / math.log(MAX_DISTANCE / max_exact)
                                  * (half - max_exact)).astype(jnp.int32)
    large = jnp.minimum(large, half - 1)
    return ret + jnp.where(n < max_exact, n, large)


def _bias_rows(table, half_window, stride):
    n_rel = 2 * Q_BLK - 1 + 4 * half_window
    width = -(-n_rel // LANES) * LANES
    rel = jnp.arange(width) - (Q_BLK - 1 + 2 * half_window)
    bucket = _t5_bucket(rel * stride)[None, :]
    tab = table.astype(F32) * LOG2E
    rows = jnp.zeros((table.shape[1], width), F32)
    for b in range(N_BUCKETS):
        rows = jnp.where(bucket == b, tab[b][:, None], rows)
    return jnp.where((jnp.abs(rel) <= half_window)[None, :], rows, NEG_INF)


_A_HEAD_ORDER = tuple(h for p in range(A_Q_HEADS // 2) for h in (p, p + A_Q_HEADS // 2))


def kernel(x, norm_gain, w_in, q_norm_a, k_norm_a, q_norm_b, k_norm_b, sink_a, rel_bias,
           w_branch_a, w_branch_b, b_merge, w_out):
    bsz, seq, d = x.shape
    n_g = len(B_GROUPS)
    scale = HEAD_DIM ** -0.5

    expand = jnp.asarray(np.kron(np.eye(LANES, B_HEADS), np.ones((1, HEAD_DIM))), BF16)
    for layer in range(norm_gain.shape[0]):
        w = w_in[layer].astype(BF16)
        wa = jnp.concatenate([w_branch_a[layer][h * HEAD_DIM:(h + 1) * HEAD_DIM] for h in _A_HEAD_ORDER],
                             axis=0).astype(BF16)
        wb = w_branch_b[layer].astype(BF16)
        wo = w_out[layer].astype(BF16)
        gqa = jnp.tile(q_norm_a[layer] * (scale * LOG2E), A_WIDTH // HEAD_DIM)[None]
        gka = jnp.tile(k_norm_a[layer], A_KV_HEADS)[None]
        gqb = jnp.tile(q_norm_b[layer] * (scale * LOG2E), B_HEADS)[None]
        gkb = jnp.tile(k_norm_b[layer], B_HEADS)[None]
        bm = b_merge[layer].reshape(1, -1)
        sink = sink_a[layer][np.asarray(_A_HEAD_ORDER)] * LOG2E

        x2 = x.reshape(bsz * seq, d)
        qa, ka, va, qbs, kbs, vbs, sga, sgb, gates = _in_proj(
            x2, norm_gain[layer][None], w, gqa, gka, gqb, gkb, bm, tm=512)

        bias_a = _bias_rows(rel_bias[:, :A_Q_HEADS][:, np.asarray(_A_HEAD_ORDER)], A_WINDOW, 1)
        (oa,) = _band_attn(qa.reshape(bsz, seq, A_WIDTH), ka.reshape(bsz, seq, PAIR),
                           va.reshape(bsz, seq, PAIR), bias_a, sink,
                           seq_len=seq, tq=1024, half_window=A_WINDOW, stack=A_Q_HEADS // 2, n_cls=1,
                           unroll=8, want_lse=False)

        obs, lses = [], []
        for g, (window, dil) in enumerate(B_GROUPS):
            c0 = A_Q_HEADS + g * B_HEADS
            hw = window // (2 * dil)
            sub = seq // dil
            bias_g = _bias_rows(rel_bias[:, c0:c0 + B_HEADS], hw, dil)
            view = (bsz, sub, dil * B_WIDTH)
            o_g, lse_g = _band_attn(qbs[g].reshape(view), kbs[g].reshape(view), vbs[g].reshape(view),
                                    bias_g, None, seq_len=sub, tq=min(sub, 1024), half_window=hw, stack=1,
                                    n_cls=min(dil, 4), unroll=max(1, 4 // dil), want_lse=True)
            obs.append(o_g.reshape(bsz * sub, dil * B_WIDTH))
            lses.append(lse_g.reshape(bsz * sub, dil * LANES))

        y2 = _merge_out(x2, oa.reshape(bsz * seq, A_WIDTH), sga, obs, lses, sgb, gates,
                        expand, wa, wb, wo, tm=512)
        x = y2.reshape(bsz, seq, d)
    return x
```

```python
import functools
import math

import jax
import jax.numpy as jnp
import numpy as np
from jax import lax
from jax.experimental import pallas as pl
from jax.experimental.pallas import tpu as pltpu

D_MODEL = 1024
HEAD_DIM = 64
A_Q_HEADS = 8
A_KV_HEADS = 2
A_WINDOW = 128
B_GROUPS = ((128, 1), (512, 4), (2048, 16))
B_HEADS = 8
A_WIDTH = A_Q_HEADS * HEAD_DIM
B_WIDTH = B_HEADS * HEAD_DIM
N_BUCKETS = 32
MAX_DISTANCE = 1024
EPS = 1e-6
NEG_INF = -1e30
LOG2E = 1.0 / math.log(2.0)

LANES = 128
PAIR = 2 * HEAD_DIM
Q_BLK = 128
SEG = 256
VMEM_LIMIT = 56 * 1024 * 1024

BF16 = jnp.bfloat16
F32 = jnp.float32


def _head_norm(y, gain):
    lo = lax.broadcasted_iota(jnp.int32, (1, PAIR), 1) < HEAD_DIM
    parts = []
    for c in range(0, y.shape[1], PAIR):
        yc = y[:, c:c + PAIR]
        sq = yc * yc
        s_lo = jnp.sum(jnp.where(lo, sq, 0.0), axis=-1, keepdims=True)
        s_hi = jnp.sum(jnp.where(lo, 0.0, sq), axis=-1, keepdims=True)
        ms = jnp.where(lo, s_lo, s_hi) * (1.0 / HEAD_DIM)
        parts.append(yc * lax.rsqrt(ms + EPS))
    yn = parts[0] if len(parts) == 1 else jnp.concatenate(parts, axis=1)
    return yn * gain


def _silu(y):
    return y * (1.0 / (1.0 + jnp.exp(-y)))


def _pair_order(y):
    lo = lax.broadcasted_iota(jnp.int32, (1, PAIR), 1) < HEAD_DIM
    n_pairs = A_Q_HEADS // 2
    cols = [y[:, g * PAIR:(g + 1) * PAIR] for g in range(n_pairs)]
    swapped = [pltpu.roll(col, HEAD_DIM, 1) for col in cols]
    out = []
    for p in range(n_pairs):
        first, second = p, p + n_pairs
        lo_src = cols[first // 2] if first % 2 == 0 else swapped[first // 2]
        hi_src = cols[second // 2] if second % 2 == 1 else swapped[second // 2]
        out.append(jnp.where(lo, lo_src, hi_src))
    return jnp.concatenate(out, axis=1)


def _store_by_class(y, dil, scr_ref, out_ref, col0, width):
    rows, w = y.shape
    if dil == 1:
        out_ref[:, col0:col0 + w] = y.astype(out_ref.dtype)
        return
    slabs = w // LANES
    for s in range(slabs):
        scr_ref[s] = y[:, s * LANES:(s + 1) * LANES]
    for c in range(dil):
        for s in range(slabs):
            col = c * width + col0 + s * LANES
            out_ref[:, col:col + LANES] = scr_ref[s, pl.ds(c, rows // dil, stride=dil), :].astype(out_ref.dtype)


def _load_by_class(ref, dil, scr_ref, width):
    if dil == 1:
        return ref[...].astype(F32)
    sub_rows = ref.shape[0]
    slabs = width // LANES
    for c in range(dil):
        for s in range(slabs):
            col = c * width + s * LANES
            scr_ref[s, pl.ds(c, sub_rows, stride=dil), :] = ref[:, col:col + LANES].astype(F32)
    parts = [scr_ref[s] for s in range(slabs)]
    return parts[0] if slabs == 1 else jnp.concatenate(parts, axis=1)


def _in_proj_kernel(x_ref, ng_ref, w_ref, gqa_ref, gka_ref, gqb_ref, gkb_ref, bm_ref,
                    qa_ref, ka_ref, va_ref, qb_refs, kb_refs, vb_refs, sga_ref, sgb_ref, gate_ref,
                    scr_refs):
    x = x_ref[...]
    ms = jnp.mean(x * x, axis=-1, keepdims=True)
    h = ((x * lax.rsqrt(ms + EPS)) * ng_ref[...]).astype(BF16)

    def segments(c0, width):
        for off in range(0, width, SEG):
            yield off, jnp.dot(h, w_ref[:, c0 + off:c0 + off + SEG], preferred_element_type=F32)

    c = 0
    qa = [_head_norm(y, gqa_ref[:, off:off + SEG]) for off, y in segments(c, A_WIDTH)]
    qa_ref[...] = _pair_order(jnp.concatenate(qa, axis=1)).astype(BF16)
    c += A_WIDTH
    for _, y in segments(c, 2 * PAIR):
        ka_ref[...] = _head_norm(y[:, :PAIR], gka_ref[...]).astype(BF16)
        va_ref[...] = y[:, PAIR:].astype(BF16)
    c += 2 * PAIR
    n_scr = 0
    for refs, gain_ref in ((qb_refs, gqb_ref), (kb_refs, gkb_ref), (vb_refs, None)):
        for g, (_, dil) in enumerate(B_GROUPS):
            for off, y in segments(c, B_WIDTH):
                if gain_ref is not None:
                    y = _head_norm(y, gain_ref[:, off:off + SEG])
                _store_by_class(y, dil, scr_refs[n_scr % len(scr_refs)], refs[g], off, B_WIDTH)
                n_scr += dil > 1
            c += B_WIDTH
    sga = [_silu(y) for _, y in segments(c, A_WIDTH)]
    sga_ref[...] = _pair_order(jnp.concatenate(sga, axis=1)).astype(BF16)
    c += A_WIDTH
    for off, y in segments(c, B_WIDTH):
        sgb_ref[:, off:off + SEG] = _silu(y).astype(BF16)
    c += B_WIDTH
    for off, y in segments(c, 2 * D_MODEL):
        y = y + bm_ref[:, off:off + SEG]
        gate_ref[:, off:off + SEG] = (1.0 / (1.0 + jnp.exp(-y))).astype(BF16)


def _in_proj(x2, ng, w, gqa, gka, gqb, gkb, bm, tm):
    m = x2.shape[0]

    def rows(width):
        return pl.BlockSpec((tm, width), lambda i: (i, 0))

    def whole(a):
        return pl.BlockSpec(a.shape, lambda i: (0,) * a.ndim)

    def out(width):
        return jax.ShapeDtypeStruct((m, width), BF16)

    b_shape = [jax.ShapeDtypeStruct((m // dil, dil * B_WIDTH), BF16) for _, dil in B_GROUPS]
    b_spec = [pl.BlockSpec((tm // dil, dil * B_WIDTH), lambda i: (i, 0)) for _, dil in B_GROUPS]
    out_shape = (out(A_WIDTH), out(PAIR), out(PAIR), b_shape, b_shape, b_shape,
                 out(A_WIDTH), out(B_WIDTH), out(2 * D_MODEL))
    out_specs = (rows(A_WIDTH), rows(PAIR), rows(PAIR), b_spec, b_spec, b_spec,
                 rows(A_WIDTH), rows(B_WIDTH), rows(2 * D_MODEL))
    scratch = [[pltpu.VMEM((SEG // LANES, tm, LANES), F32)] * 2]
    return pl.pallas_call(
        _in_proj_kernel,
        grid=(m // tm,),
        in_specs=[rows(D_MODEL), whole(ng), pl.BlockSpec(memory_space=pltpu.VMEM),
                  whole(gqa), whole(gka), whole(gqb), whole(gkb), whole(bm)],
        out_specs=out_specs,
        out_shape=out_shape,
        scratch_shapes=scratch,
        compiler_params=pltpu.CompilerParams(vmem_limit_bytes=VMEM_LIMIT),
        name="in_proj",
    )(x2, ng, w, gqa, gka, gqb, gkb, bm)


def _init_bias_tiles(bias_row_ref, bias_ref, *, half_window, n_heads, stack):
    tk = Q_BLK + 2 * half_window
    width = bias_row_ref.shape[1]
    for li, lead in enumerate((0, half_window, 2 * half_window)):
        shift = width - (Q_BLK - 1 + 2 * half_window - lead)
        for head in range(n_heads):
            base = jnp.broadcast_to(bias_row_ref[head:head + 1, :], (Q_BLK, width))
            tile = pltpu.roll(base, shift, 1, stride=1, stride_axis=0)
            grp, idx = divmod(head, 2 * stack)
            bias_ref[li, grp, idx * Q_BLK:(idx + 1) * Q_BLK, :] = tile[:, :tk]


def _attn_block(blk, load_q, load_kv, bias_ref, sink_ref, store_out, store_stats, *,
                seq_len, half_window, n_cls, q_pairs, kv_pairs, stack):
    tk = Q_BLK + 2 * half_window
    n_blk = seq_len // Q_BLK
    lo = lax.broadcasted_iota(jnp.int32, (1, PAIR), 1) < HEAD_DIM
    lane_l = lax.broadcasted_iota(jnp.int32, (1, LANES), 1)
    ws = pl.multiple_of(jnp.clip(blk * Q_BLK - half_window, 0, seq_len - tk), HEAD_DIM)
    sel = (blk > 0).astype(jnp.int32) + (blk == n_blk - 1).astype(jnp.int32)
    ones = jnp.ones((tk, PAIR), BF16)
    for c in range(n_cls):
        st_tile = jnp.where(lane_l < HEAD_DIM, jnp.zeros((Q_BLK, LANES), F32), 1.0)
        for grp in range(q_pairs // stack):
            k, v = load_kv(ws, (c * kv_pairs + grp * stack * kv_pairs // q_pairs) * PAIR)
            qcols = [(c * q_pairs + grp * stack + i) * PAIR for i in range(stack)]
            qs = []
            for qcol in qcols:
                q = load_q(qcol)
                zero = jnp.zeros_like(q)
                qs += [jnp.where(lo, q, zero), jnp.where(lo, zero, q)]
            s = lax.dot_general(jnp.concatenate(qs, axis=0), k, (((1,), (1,)), ((), ())),
                                preferred_element_type=F32)
            s = s + bias_ref[sel, grp]
            heads = range(2 * stack * grp, 2 * stack * (grp + 1))
            ms, es = [], []
            for i, head in enumerate(heads):
                s_h = s[i * Q_BLK:(i + 1) * Q_BLK]
                m_h = jnp.max(s_h, axis=-1, keepdims=True)
                if sink_ref is not None:
                    m_h = jnp.maximum(m_h, sink_ref[head])
                ms.append(m_h)
                es.append(jnp.exp2(s_h - m_h).astype(BF16))
            pv = jnp.dot(jnp.concatenate(es, axis=0), jnp.concatenate([v, ones], axis=1),
                         preferred_element_type=F32)
            outs = []
            for i, head in enumerate(heads):
                l_h = pv[i * Q_BLK:(i + 1) * Q_BLK, PAIR:]
                if sink_ref is not None:
                    l_h = l_h + jnp.exp2(sink_ref[head] - ms[i])
                if store_stats is not None:
                    outs.append(pv[i * Q_BLK:(i + 1) * Q_BLK, :PAIR])
                    st_tile = jnp.where(lane_l == head, ms[i], st_tile)
                    st_tile = jnp.where(lane_l == HEAD_DIM + head, l_h, st_tile)
                else:
                    outs.append(pv[i * Q_BLK:(i + 1) * Q_BLK, :PAIR] * (1.0 / l_h))
            for i, qcol in enumerate(qcols):
                store_out(qcol, jnp.where(lo, outs[2 * i], outs[2 * i + 1]))
        if store_stats is not None:
            store_stats(c, st_tile)


def _band_attn_kernel(q_ref, k_ref, v_ref, bias_row_ref, o_ref, st_ref, bias_ref, *,
                      seq_len, tq, half_window, n_cls, q_pairs, unroll):
    t = pl.program_id(2)

    @pl.when((pl.program_id(0) == 0) & (pl.program_id(1) == 0) & (t == 0))
    def _():
        _init_bias_tiles(bias_row_ref, bias_ref, half_window=half_window, n_heads=2 * q_pairs, stack=1)

    tk = Q_BLK + 2 * half_window

    def block(j, carry):
        row = pl.multiple_of(j * Q_BLK, Q_BLK)

        def store_out(col, x):
            o_ref[0, pl.ds(row, Q_BLK), col:col + PAIR] = x.astype(BF16)

        def store_stats(c, tile):
            st_ref[0, pl.ds(row, Q_BLK), c * LANES:(c + 1) * LANES] = tile

        _attn_block(t * (tq // Q_BLK) + j,
                    lambda col: q_ref[0, pl.ds(row, Q_BLK), col:col + PAIR],
                    lambda ws, col: (k_ref[0, pl.ds(ws, tk), col:col + PAIR],
                                     v_ref[0, pl.ds(ws, tk), col:col + PAIR]),
                    bias_ref, None, store_out, store_stats, seq_len=seq_len, half_window=half_window,
                    n_cls=n_cls, q_pairs=q_pairs, kv_pairs=q_pairs, stack=1)
        return carry

    lax.fori_loop(0, tq // Q_BLK, block, 0, unroll=unroll)


def _band_attn(q, k, v, bias_rows, *, seq_len, tq, half_window, n_cls, unroll):
    n_rows, _, cols = q.shape
    tk = Q_BLK + 2 * half_window
    q_pairs = bias_rows.shape[0] // 2
    n_col = cols // (q_pairs * PAIR)
    assert k.shape == q.shape and v.shape == q.shape and bias_rows.shape[1] % LANES == 0
    assert seq_len % tq == 0 and tq % Q_BLK == 0 and n_col % n_cls == 0

    blk = n_cls * q_pairs * PAIR
    rows_spec = pl.BlockSpec((1, tq, blk), lambda r, c, t: (r, t, c))
    seq_spec = pl.BlockSpec((1, seq_len, blk), lambda r, c, t: (r, 0, c))
    kern = functools.partial(_band_attn_kernel, seq_len=seq_len, tq=tq, half_window=half_window,
                             n_cls=n_cls, q_pairs=q_pairs, unroll=unroll)
    return pl.pallas_call(
        kern,
        grid=(n_rows, n_col // n_cls, seq_len // tq),
        in_specs=[rows_spec, seq_spec, seq_spec, pl.BlockSpec(bias_rows.shape, lambda r, c, t: (0, 0))],
        out_specs=[rows_spec, pl.BlockSpec((1, tq, n_cls * LANES), lambda r, c, t: (r, t, c))],
        out_shape=[jax.ShapeDtypeStruct(q.shape, BF16),
                   jax.ShapeDtypeStruct((n_rows, seq_len, n_col * LANES), F32)],
        scratch_shapes=[pltpu.VMEM((3, q_pairs, 2 * Q_BLK, tk), F32)],
        compiler_params=pltpu.CompilerParams(vmem_limit_bytes=VMEM_LIMIT,
                                             dimension_semantics=("arbitrary",) * 3),
        name=f"band_attn_len{seq_len}",
    )(q, k, v, bias_rows)


def _merge_out_kernel(x_ref, qa_ref, ka_ref, va_ref, bias_row_ref, sink_ref, sga_ref, ob_refs, st_refs,
                      sgb_ref, gate_ref, expand_ref, wa_ref, wb_ref, wo_ref, y_ref,
                      bias_ref, ya_ref, scr_o_refs, scr_l_refs, *, seq_len):
    i = pl.program_id(0)
    tm = x_ref.shape[0]
    stack = A_Q_HEADS // 2

    @pl.when(i == 0)
    def _():
        _init_bias_tiles(bias_row_ref, bias_ref, half_window=A_WINDOW, n_heads=A_Q_HEADS, stack=stack)

    first_blk = lax.rem(i, seq_len // tm) * (tm // Q_BLK)
    tk = Q_BLK + 2 * A_WINDOW
    for j in range(tm // Q_BLK):
        rows = slice(j * Q_BLK, (j + 1) * Q_BLK)

        def store_out(col, x, rows=rows):
            ya_ref[rows, col:col + PAIR] = (x * sga_ref[rows, col:col + PAIR].astype(F32)).astype(BF16)

        _attn_block(first_blk + j,
                    lambda col, rows=rows: qa_ref[rows, col:col + PAIR],
                    lambda ws, col: (ka_ref[0, pl.ds(ws, tk), col:col + PAIR],
                                     va_ref[0, pl.ds(ws, tk), col:col + PAIR]),
                    bias_ref, sink_ref, store_out, None, seq_len=seq_len, half_window=A_WINDOW, n_cls=1,
                    q_pairs=stack, kv_pairs=A_KV_HEADS // 2, stack=stack)

    dils = [dil for _, dil in B_GROUPS]
    stats = [_load_by_class(r, dil, s, LANES) for r, dil, s in zip(st_refs, dils, scr_l_refs)]
    is_head = lax.broadcasted_iota(jnp.int32, (1, LANES), 1) < HEAD_DIM
    m = functools.reduce(jnp.maximum, stats)
    es = [jnp.exp2(st - m) for st in stats]
    den = functools.reduce(lambda a, b: a + b, [e * pltpu.roll(st, HEAD_DIM, 1) for e, st in zip(es, stats)])
    inv = jnp.where(is_head, 1.0 / den, 0.0)
    yb = None
    for e, o_ref, dil, scr in zip(es, ob_refs, dils, scr_o_refs):
        alpha = jnp.dot((e * inv).astype(BF16), expand_ref[...], preferred_element_type=F32)
        term = alpha * _load_by_class(o_ref, dil, scr, B_WIDTH)
        yb = term if yb is None else yb + term
    yb = (yb * sgb_ref[...].astype(F32)).astype(BF16)
    br_a = jnp.dot(ya_ref[...], wa_ref[...], preferred_element_type=F32)
    br_b = jnp.dot(yb, wb_ref[...], preferred_element_type=F32)
    merged = (gate_ref[:, :D_MODEL].astype(F32) * br_a
              + gate_ref[:, D_MODEL:].astype(F32) * br_b).astype(BF16)
    y_ref[...] = x_ref[...] + jnp.dot(merged, wo_ref[...], preferred_element_type=F32)


def _merge_out(x2, qa, ka, va, bias_rows, sink, sga, obs, stats, sgb, gates, expand, wa, wb, wo, *, tm, seq_len):
    m = x2.shape[0]
    tiles_per_seq = seq_len // tm

    def rows(width):
        return pl.BlockSpec((tm, width), lambda i: (i, 0))

    def whole(a):
        return pl.BlockSpec(a.shape, lambda i: (0,) * a.ndim)

    def by_class(width):
        return [pl.BlockSpec((tm // dil, dil * width), lambda i: (i, 0)) for _, dil in B_GROUPS]

    seq_kv = pl.BlockSpec((1, seq_len, PAIR), lambda i: (i // tiles_per_seq, 0, 0))
    n_g = len(B_GROUPS)
    stack = A_Q_HEADS // 2
    scratch = [pltpu.VMEM((3, 1, 2 * stack * Q_BLK, Q_BLK + 2 * A_WINDOW), F32),
               pltpu.VMEM((tm, A_WIDTH), BF16),
               [pltpu.VMEM((B_WIDTH // LANES, tm, LANES), F32)] * n_g,
               [pltpu.VMEM((1, tm, LANES), F32)] * n_g]
    return pl.pallas_call(
        functools.partial(_merge_out_kernel, seq_len=seq_len),
        grid=(m // tm,),
        in_specs=[rows(D_MODEL), rows(A_WIDTH), seq_kv, seq_kv, whole(bias_rows),
                  pl.BlockSpec(memory_space=pltpu.SMEM), rows(A_WIDTH), by_class(B_WIDTH),
                  by_class(LANES), rows(B_WIDTH), rows(2 * D_MODEL),
                  whole(expand), whole(wa), whole(wb), whole(wo)],
        out_specs=rows(D_MODEL),
        out_shape=jax.ShapeDtypeStruct((m, D_MODEL), F32),
        scratch_shapes=scratch,
        compiler_params=pltpu.CompilerParams(vmem_limit_bytes=VMEM_LIMIT, dimension_semantics=("arbitrary",)),
        name="merge_out",
    )(x2, qa, ka, va, bias_rows, sink.astype(F32), sga, obs, stats, sgb, gates, expand, wa, wb, wo)


def _t5_bucket(rel):
    half = N_BUCKETS // 2
    max_exact = half // 2
    ret = (rel > 0).astype(jnp.int32) * half
    n = jnp.abs(rel)
    nf = jnp.maximum(n, max_exact).astype(jnp.float32)
    large = max_exact + jnp.floor(jnp.log(nf / max_exact) / math.log(MAX_DISTANCE / max_exact)
                                  * (half - max_exact)).astype(jnp.int32)
    large = jnp.minimum(large, half - 1)
    return ret + jnp.where(n < max_exact, n, large)


def _bias_rows(table, half_window, stride):
    n_rel = 2 * Q_BLK - 1 + 4 * half_window
    width = -(-n_rel // LANES) * LANES
    rel = jnp.arange(width) - (Q_BLK - 1 + 2 * half_window)
    bucket = _t5_bucket(rel * stride)[None, :]
    tab = table.astype(F32) * LOG2E
    rows = jnp.zeros((table.shape[1], width), F32)
    for b in range(N_BUCKETS):
        rows = jnp.where(bucket == b, tab[b][:, None], rows)
    return jnp.where((jnp.abs(rel) <= half_window)[None, :], rows, NEG_INF)


_A_HEAD_ORDER = tuple(h for p in range(A_Q_HEADS // 2) for h in (p, p + A_Q_HEADS // 2))


def kernel(x, norm_gain, w_in, q_norm_a, k_norm_a, q_norm_b, k_norm_b, sink_a, rel_bias,
           w_branch_a, w_branch_b, b_merge, w_out):
    bsz, seq, d = x.shape
    scale = HEAD_DIM ** -0.5

    expand = jnp.asarray(np.kron(np.eye(LANES, B_HEADS), np.ones((1, HEAD_DIM))), BF16)
    for layer in range(norm_gain.shape[0]):
        w = w_in[layer].astype(BF16)
        wa = jnp.concatenate([w_branch_a[layer][h * HEAD_DIM:(h + 1) * HEAD_DIM] for h in _A_HEAD_ORDER],
                             axis=0).astype(BF16)
        wb = w_branch_b[layer].astype(BF16)
        wo = w_out[layer].astype(BF16)
        gqa = jnp.tile(q_norm_a[layer] * (scale * LOG2E), A_WIDTH // HEAD_DIM)[None]
        gka = jnp.tile(k_norm_a[layer], A_KV_HEADS)[None]
        gqb = jnp.tile(q_norm_b[layer] * (scale * LOG2E), B_HEADS)[None]
        gkb = jnp.tile(k_norm_b[layer], B_HEADS)[None]
        bm = b_merge[layer].reshape(1, -1)
        sink = sink_a[layer][np.asarray(_A_HEAD_ORDER)] * LOG2E

        x2 = x.reshape(bsz * seq, d)
        qa, ka, va, qbs, kbs, vbs, sga, sgb, gates = _in_proj(
            x2, norm_gain[layer][None], w, gqa, gka, gqb, gkb, bm, tm=512)

        obs, stats = [], []
        for g, (window, dil) in enumerate(B_GROUPS):
            c0 = A_Q_HEADS + g * B_HEADS
            hw = window // (2 * dil)
            sub = seq // dil
            bias_g = _bias_rows(rel_bias[:, c0:c0 + B_HEADS], hw, dil)
            view = (bsz, sub, dil * B_WIDTH)
            o_g, st_g = _band_attn(qbs[g].reshape(view), kbs[g].reshape(view), vbs[g].reshape(view), bias_g,
                                   seq_len=sub, tq=min(sub, 1024), half_window=hw,
                                   n_cls=min(dil, 4), unroll=max(1, 4 // dil))
            obs.append(o_g.reshape(bsz * sub, dil * B_WIDTH))
            stats.append(st_g.reshape(bsz * sub, dil * LANES))

        bias_a = _bias_rows(rel_bias[:, :A_Q_HEADS][:, np.asarray(_A_HEAD_ORDER)], A_WINDOW, 1)
        y2 = _merge_out(x2, qa, ka.reshape(bsz, seq, PAIR), va.reshape(bsz, seq, PAIR), bias_a, sink,
                        sga, obs, stats, sgb, gates, expand, wa, wb, wo, tm=512, seq_len=seq)
        x = y2.reshape(bsz, seq, d)
    return x
```

```python
import functools
import math

import jax
import jax.numpy as jnp
import numpy as np
from jax import lax
from jax.experimental import pallas as pl
from jax.experimental.pallas import tpu as pltpu

D_MODEL = 1024
HEAD_DIM = 64
A_Q_HEADS = 8
A_KV_HEADS = 2
A_WINDOW = 128
B_GROUPS = ((128, 1), (512, 4), (2048, 16))
B_HEADS = 8
A_WIDTH = A_Q_HEADS * HEAD_DIM
B_WIDTH = B_HEADS * HEAD_DIM
N_BUCKETS = 32
MAX_DISTANCE = 1024
EPS = 1e-6
NEG_INF = -1e30
LOG2E = 1.0 / math.log(2.0)

LANES = 128
PAIR = 2 * HEAD_DIM
Q_BLK = 128
SEG = 256
VMEM_LIMIT = 56 * 1024 * 1024

BF16 = jnp.bfloat16
F32 = jnp.float32


def _head_norm(y, gain):
    lo = lax.broadcasted_iota(jnp.int32, (1, PAIR), 1) < HEAD_DIM
    parts = []
    for c in range(0, y.shape[1], PAIR):
        yc = y[:, c:c + PAIR]
        sq = yc * yc
        s_lo = jnp.sum(jnp.where(lo, sq, 0.0), axis=-1, keepdims=True)
        s_hi = jnp.sum(jnp.where(lo, 0.0, sq), axis=-1, keepdims=True)
        ms = jnp.where(lo, s_lo, s_hi) * (1.0 / HEAD_DIM)
        parts.append(yc * lax.rsqrt(ms + EPS))
    yn = parts[0] if len(parts) == 1 else jnp.concatenate(parts, axis=1)
    return yn * gain


def _silu(y):
    return y * (1.0 / (1.0 + jnp.exp(-y)))


def _pair_order(y):
    lo = lax.broadcasted_iota(jnp.int32, (1, PAIR), 1) < HEAD_DIM
    n_pairs = A_Q_HEADS // 2
    cols = [y[:, g * PAIR:(g + 1) * PAIR] for g in range(n_pairs)]
    swapped = [pltpu.roll(col, HEAD_DIM, 1) for col in cols]
    out = []
    for p in range(n_pairs):
        first, second = p, p + n_pairs
        lo_src = cols[first // 2] if first % 2 == 0 else swapped[first // 2]
        hi_src = cols[second // 2] if second % 2 == 1 else swapped[second // 2]
        out.append(jnp.where(lo, lo_src, hi_src))
    return jnp.concatenate(out, axis=1)


def _store_by_class(y, dil, scr_ref, out_ref, col0, width):
    rows, w = y.shape
    if dil == 1:
        out_ref[:, col0:col0 + w] = y.astype(out_ref.dtype)
        return
    slabs = w // LANES
    for s in range(slabs):
        scr_ref[s] = y[:, s * LANES:(s + 1) * LANES]
    for c in range(dil):
        for s in range(slabs):
            col = c * width + col0 + s * LANES
            out_ref[:, col:col + LANES] = scr_ref[s, pl.ds(c, rows // dil, stride=dil), :].astype(out_ref.dtype)


def _load_by_class(ref, dil, scr_ref, width):
    if dil == 1:
        return ref[...].astype(F32)
    sub_rows = ref.shape[0]
    slabs = width // LANES
    for c in range(dil):
        for s in range(slabs):
            col = c * width + s * LANES
            scr_ref[s, pl.ds(c, sub_rows, stride=dil), :] = ref[:, col:col + LANES].astype(F32)
    parts = [scr_ref[s] for s in range(slabs)]
    return parts[0] if slabs == 1 else jnp.concatenate(parts, axis=1)


def _in_proj_kernel(x_ref, ng_ref, w_ref, gqa_ref, gka_ref, gqb_ref, gkb_ref, bm_ref,
                    qa_ref, ka_ref, va_ref, qb_refs, kb_refs, vb_refs, sga_ref, sgb_ref, gate_ref,
                    scr_refs):
    x = x_ref[...]
    ms = jnp.mean(x * x, axis=-1, keepdims=True)
    h = ((x * lax.rsqrt(ms + EPS)) * ng_ref[...]).astype(BF16)

    def segments(c0, width):
        for off in range(0, width, SEG):
            yield off, jnp.dot(h, w_ref[:, c0 + off:c0 + off + SEG], preferred_element_type=F32)

    c = 0
    qa = [_head_norm(y, gqa_ref[:, off:off + SEG]) for off, y in segments(c, A_WIDTH)]
    qa_ref[...] = _pair_order(jnp.concatenate(qa, axis=1)).astype(BF16)
    c += A_WIDTH
    for _, y in segments(c, 2 * PAIR):
        ka_ref[...] = _head_norm(y[:, :PAIR], gka_ref[...]).astype(BF16)
        va_ref[...] = y[:, PAIR:].astype(BF16)
    c += 2 * PAIR
    n_scr = 0
    for refs, gain_ref in ((qb_refs, gqb_ref), (kb_refs, gkb_ref), (vb_refs, None)):
        for g, (_, dil) in enumerate(B_GROUPS):
            for off, y in segments(c, B_WIDTH):
                if gain_ref is not None:
                    y = _head_norm(y, gain_ref[:, off:off + SEG])
                _store_by_class(y, dil, scr_refs[n_scr % len(scr_refs)], refs[g], off, B_WIDTH)
                n_scr += dil > 1
            c += B_WIDTH
    sga = [_silu(y) for _, y in segments(c, A_WIDTH)]
    sga_ref[...] = _pair_order(jnp.concatenate(sga, axis=1)).astype(BF16)
    c += A_WIDTH
    for off, y in segments(c, B_WIDTH):
        sgb_ref[:, off:off + SEG] = _silu(y).astype(BF16)
    c += B_WIDTH
    for off, y in segments(c, 2 * D_MODEL):
        y = y + bm_ref[:, off:off + SEG]
        gate_ref[:, off:off + SEG] = (1.0 / (1.0 + jnp.exp(-y))).astype(BF16)


def _in_proj(x2, ng, w, gqa, gka, gqb, gkb, bm, tm):
    m = x2.shape[0]

    def rows(width):
        return pl.BlockSpec((tm, width), lambda i: (i, 0))

    def whole(a):
        return pl.BlockSpec(a.shape, lambda i: (0,) * a.ndim)

    def out(width):
        return jax.ShapeDtypeStruct((m, width), BF16)

    b_shape = [jax.ShapeDtypeStruct((m // dil, dil * B_WIDTH), BF16) for _, dil in B_GROUPS]
    b_spec = [pl.BlockSpec((tm // dil, dil * B_WIDTH), lambda i: (i, 0)) for _, dil in B_GROUPS]
    out_shape = (out(A_WIDTH), out(PAIR), out(PAIR), b_shape, b_shape, b_shape,
                 out(A_WIDTH), out(B_WIDTH), out(2 * D_MODEL))
    out_specs = (rows(A_WIDTH), rows(PAIR), rows(PAIR), b_spec, b_spec, b_spec,
                 rows(A_WIDTH), rows(B_WIDTH), rows(2 * D_MODEL))
    scratch = [[pltpu.VMEM((SEG // LANES, tm, LANES), F32)] * 2]
    return pl.pallas_call(
        _in_proj_kernel,
        grid=(m // tm,),
        in_specs=[rows(D_MODEL), whole(ng), pl.BlockSpec(memory_space=pltpu.VMEM),
                  whole(gqa), whole(gka), whole(gqb), whole(gkb), whole(bm)],
        out_specs=out_specs,
        out_shape=out_shape,
        scratch_shapes=scratch,
        compiler_params=pltpu.CompilerParams(vmem_limit_bytes=VMEM_LIMIT),
        name="in_proj",
    )(x2, ng, w, gqa, gka, gqb, gkb, bm)


def _init_bias_tiles(bias_row_ref, bias_ref, *, half_window, n_heads, stack):
    tk = Q_BLK + 2 * half_window
    width = bias_row_ref.shape[1]
    for li, lead in enumerate((0, half_window, 2 * half_window)):
        shift = width - (Q_BLK - 1 + 2 * half_window - lead)
        for head in range(n_heads):
            base = jnp.broadcast_to(bias_row_ref[head:head + 1, :], (Q_BLK, width))
            tile = pltpu.roll(base, shift, 1, stride=1, stride_axis=0)
            grp, idx = divmod(head, 2 * stack)
            bias_ref[li, grp, idx * Q_BLK:(idx + 1) * Q_BLK, :] = tile[:, :tk]


def _attn_block(blk, load_q, load_kv, bias_ref, sink_ref, store_out, store_stats, *,
                seq_len, half_window, n_cls, q_pairs, kv_pairs, stack):
    tk = Q_BLK + 2 * half_window
    n_blk = seq_len // Q_BLK
    lo = lax.broadcasted_iota(jnp.int32, (1, PAIR), 1) < HEAD_DIM
    lane_l = lax.broadcasted_iota(jnp.int32, (1, LANES), 1)
    ws = pl.multiple_of(jnp.clip(blk * Q_BLK - half_window, 0, seq_len - tk), HEAD_DIM)
    sel = (blk > 0).astype(jnp.int32) + (blk == n_blk - 1).astype(jnp.int32)
    ones = jnp.ones((tk, PAIR), BF16)
    for c in range(n_cls):
        st_tile = jnp.where(lane_l < HEAD_DIM, jnp.zeros((Q_BLK, LANES), F32), 1.0)
        for grp in range(q_pairs // stack):
            k, v = load_kv(ws, (c * kv_pairs + grp * stack * kv_pairs // q_pairs) * PAIR)
            qcols = [(c * q_pairs + grp * stack + i) * PAIR for i in range(stack)]
            qs = []
            for qcol in qcols:
                q = load_q(qcol)
                zero = jnp.zeros_like(q)
                qs += [jnp.where(lo, q, zero), jnp.where(lo, zero, q)]
            s = lax.dot_general(jnp.concatenate(qs, axis=0), k, (((1,), (1,)), ((), ())),
                                preferred_element_type=F32)
            s = s + bias_ref[sel, grp]
            heads = range(2 * stack * grp, 2 * stack * (grp + 1))
            ms, es = [], []
            for i, head in enumerate(heads):
                s_h = s[i * Q_BLK:(i + 1) * Q_BLK]
                m_h = jnp.max(s_h, axis=-1, keepdims=True)
                if sink_ref is not None:
                    m_h = jnp.maximum(m_h, sink_ref[head])
                ms.append(m_h)
                es.append(jnp.exp2(s_h - m_h).astype(BF16))
            pv = jnp.dot(jnp.concatenate(es, axis=0), jnp.concatenate([v, ones], axis=1),
                         preferred_element_type=F32)
            outs = []
            for i, head in enumerate(heads):
                l_h = pv[i * Q_BLK:(i + 1) * Q_BLK, PAIR:]
                if sink_ref is not None:
                    l_h = l_h + jnp.exp2(sink_ref[head] - ms[i])
                if store_stats is not None:
                    outs.append(pv[i * Q_BLK:(i + 1) * Q_BLK, :PAIR])
                    st_tile = jnp.where(lane_l == head, ms[i], st_tile)
                    st_tile = jnp.where(lane_l == HEAD_DIM + head, l_h, st_tile)
                else:
                    outs.append(pv[i * Q_BLK:(i + 1) * Q_BLK, :PAIR] * (1.0 / l_h))
            for i, qcol in enumerate(qcols):
                store_out(qcol, jnp.where(lo, outs[2 * i], outs[2 * i + 1]))
        if store_stats is not None:
            store_stats(c, st_tile)


def _band_attn_kernel(q_ref, k_ref, v_ref, bias_row_ref, o_ref, st_ref, bias_ref, *,
                      seq_len, tq, half_window, n_cls, q_pairs, unroll):
    t = pl.program_id(2)

    @pl.when((pl.program_id(0) == 0) & (pl.program_id(1) == 0) & (t == 0))
    def _():
        _init_bias_tiles(bias_row_ref, bias_ref, half_window=half_window, n_heads=2 * q_pairs, stack=1)

    tk = Q_BLK + 2 * half_window

    def block(j, carry):
        row = pl.multiple_of(j * Q_BLK, Q_BLK)

        def store_out(col, x):
            o_ref[0, pl.ds(row, Q_BLK), col:col + PAIR] = x.astype(BF16)

        def store_stats(c, tile):
            st_ref[0, pl.ds(row, Q_BLK), c * LANES:(c + 1) * LANES] = tile

        _attn_block(t * (tq // Q_BLK) + j,
                    lambda col: q_ref[0, pl.ds(row, Q_BLK), col:col + PAIR],
                    lambda ws, col: (k_ref[0, pl.ds(ws, tk), col:col + PAIR],
                                     v_ref[0, pl.ds(ws, tk), col:col + PAIR]),
                    bias_ref, None, store_out, store_stats, seq_len=seq_len, half_window=half_window,
                    n_cls=n_cls, q_pairs=q_pairs, kv_pairs=q_pairs, stack=1)
        return carry

    lax.fori_loop(0, tq // Q_BLK, block, 0, unroll=unroll)


def _band_attn(q, k, v, bias_rows, *, seq_len, tq, half_window, n_cls, unroll):
    n_rows, _, cols = q.shape
    tk = Q_BLK + 2 * half_window
    q_pairs = bias_rows.shape[0] // 2
    n_col = cols // (q_pairs * PAIR)
    assert k.shape == q.shape and v.shape == q.shape and bias_rows.shape[1] % LANES == 0
    assert seq_len % tq == 0 and tq % Q_BLK == 0 and n_col % n_cls == 0

    blk = n_cls * q_pairs * PAIR
    rows_spec = pl.BlockSpec((1, tq, blk), lambda r, c, t: (r, t, c))
    seq_spec = pl.BlockSpec((1, seq_len, blk), lambda r, c, t: (r, 0, c))
    kern = functools.partial(_band_attn_kernel, seq_len=seq_len, tq=tq, half_window=half_window,
                             n_cls=n_cls, q_pairs=q_pairs, unroll=unroll)
    return pl.pallas_call(
        kern,
        grid=(n_rows, n_col // n_cls, seq_len // tq),
        in_specs=[rows_spec, seq_spec, seq_spec, pl.BlockSpec(bias_rows.shape, lambda r, c, t: (0, 0))],
        out_specs=[rows_spec, pl.BlockSpec((1, tq, n_cls * LANES), lambda r, c, t: (r, t, c))],
        out_shape=[jax.ShapeDtypeStruct(q.shape, BF16),
                   jax.ShapeDtypeStruct((n_rows, seq_len, n_col * LANES), F32)],
        scratch_shapes=[pltpu.VMEM((3, q_pairs, 2 * Q_BLK, tk), F32)],
        compiler_params=pltpu.CompilerParams(vmem_limit_bytes=VMEM_LIMIT,
                                             dimension_semantics=("arbitrary",) * 3),
        name=f"band_attn_len{seq_len}",
    )(q, k, v, bias_rows)


def _merge_out_kernel(x_ref, qa_ref, ka_ref, va_ref, bias_row_ref, sink_ref, sga_ref, ob_refs, st_refs,
                      sgb_ref, gate_ref, expand_ref, wa_ref, wb_ref, wo_ref, y_ref,
                      bias_ref, ya_ref, scr_o_refs, scr_l_refs, *, seq_len):
    i = pl.program_id(0)
    tm = x_ref.shape[0]
    stack = A_Q_HEADS // 2

    @pl.when(i == 0)
    def _():
        _init_bias_tiles(bias_row_ref, bias_ref, half_window=A_WINDOW, n_heads=A_Q_HEADS, stack=stack)

    first_blk = lax.rem(i, seq_len // tm) * (tm // Q_BLK)
    tk = Q_BLK + 2 * A_WINDOW
    for j in range(tm // Q_BLK):
        rows = slice(j * Q_BLK, (j + 1) * Q_BLK)

        def store_out(col, x, rows=rows):
            ya_ref[rows, col:col + PAIR] = (x * sga_ref[rows, col:col + PAIR].astype(F32)).astype(BF16)

        _attn_block(first_blk + j,
                    lambda col, rows=rows: qa_ref[rows, col:col + PAIR],
                    lambda ws, col: (ka_ref[0, pl.ds(ws, tk), col:col + PAIR],
                                     va_ref[0, pl.ds(ws, tk), col:col + PAIR]),
                    bias_ref, sink_ref, store_out, None, seq_len=seq_len, half_window=A_WINDOW, n_cls=1,
                    q_pairs=stack, kv_pairs=A_KV_HEADS // 2, stack=stack)

    dils = [dil for _, dil in B_GROUPS]
    stats = [_load_by_class(r, dil, s, LANES) for r, dil, s in zip(st_refs, dils, scr_l_refs)]
    is_head = lax.broadcasted_iota(jnp.int32, (1, LANES), 1) < HEAD_DIM
    m = functools.reduce(jnp.maximum, stats)
    es = [jnp.exp2(st - m) for st in stats]
    den = functools.reduce(lambda a, b: a + b, [e * pltpu.roll(st, HEAD_DIM, 1) for e, st in zip(es, stats)])
    inv = jnp.where(is_head, 1.0 / den, 0.0)
    yb = None
    for e, o_ref, dil, scr in zip(es, ob_refs, dils, scr_o_refs):
        alpha = jnp.dot((e * inv).astype(BF16), expand_ref[...], preferred_element_type=F32)
        term = alpha * _load_by_class(o_ref, dil, scr, B_WIDTH)
        yb = term if yb is None else yb + term
    yb = (yb * sgb_ref[...].astype(F32)).astype(BF16)
    br_a = jnp.dot(ya_ref[...], wa_ref[...], preferred_element_type=F32)
    br_b = jnp.dot(yb, wb_ref[...], preferred_element_type=F32)
    merged = (gate_ref[:, :D_MODEL].astype(F32) * br_a
              + gate_ref[:, D_MODEL:].astype(F32) * br_b).astype(BF16)
    y_ref[...] = x_ref[...] + jnp.dot(merged, wo_ref[...], preferred_element_type=F32)


def _merge_out(x2, qa, ka, va, bias_rows, sink, sga, obs, stats, sgb, gates, expand, wa, wb, wo, *, tm, seq_len):
    m = x2.shape[0]
    tiles_per_seq = seq_len // tm

    def rows(width):
        return pl.BlockSpec((tm, width), lambda i: (i, 0))

    def whole(a):
        return pl.BlockSpec(a.shape, lambda i: (0,) * a.ndim)

    def by_class(width):
        return [pl.BlockSpec((tm // dil, dil * width), lambda i: (i, 0)) for _, dil in B_GROUPS]

    seq_kv = pl.BlockSpec((1, seq_len, PAIR), lambda i: (i // tiles_per_seq, 0, 0))
    n_g = len(B_GROUPS)
    stack = A_Q_HEADS // 2
    scratch = [pltpu.VMEM((3, 1, 2 * stack * Q_BLK, Q_BLK + 2 * A_WINDOW), F32),
               pltpu.VMEM((tm, A_WIDTH), BF16),
               [pltpu.VMEM((B_WIDTH // LANES, tm, LANES), F32)] * n_g,
               [pltpu.VMEM((1, tm, LANES), F32)] * n_g]
    return pl.pallas_call(
        functools.partial(_merge_out_kernel, seq_len=seq_len),
        grid=(m // tm,),
        in_specs=[rows(D_MODEL), rows(A_WIDTH), seq_kv, seq_kv, whole(bias_rows),
                  pl.BlockSpec(memory_space=pltpu.SMEM), rows(A_WIDTH), by_class(B_WIDTH),
                  by_class(LANES), rows(B_WIDTH), rows(2 * D_MODEL),
                  whole(expand), whole(wa), whole(wb), whole(wo)],
        out_specs=rows(D_MODEL),
        out_shape=jax.ShapeDtypeStruct((m, D_MODEL), F32),
        scratch_shapes=scratch,
        compiler_params=pltpu.CompilerParams(vmem_limit_bytes=VMEM_LIMIT, dimension_semantics=("arbitrary",)),
        name="merge_out",
    )(x2, qa, ka, va, bias_rows, sink.astype(F32), sga, obs, stats, sgb, gates, expand, wa, wb, wo)


def _t5_bucket(rel):
    half = N_BUCKETS // 2
    max_exact = half // 2
    ret = (rel > 0).astype(jnp.int32) * half
    n = jnp.abs(rel)
    nf = jnp.maximum(n, max_exact).astype(jnp.float32)
    large = max_exact + jnp.floor(jnp.log(nf / max_exact) / math.log(MAX_DISTANCE / max_exact)
                                  * (half - max_exact)).astype(jnp.int32)
    large = jnp.minimum(large, half - 1)
    return ret + jnp.where(n < max_exact, n, large)


def _bias_rows(table, half_window, stride):
    n_rel = 2 * Q_BLK - 1 + 4 * half_window
    width = -(-n_rel // LANES) * LANES
    rel = jnp.arange(width) - (Q_BLK - 1 + 2 * half_window)
    bucket = _t5_bucket(rel * stride)[None, :]
    tab = table.astype(F32) * LOG2E
    rows = jnp.zeros((table.shape[1], width), F32)
    for b in range(N_BUCKETS):
        rows = jnp.where(bucket == b, tab[b][:, None], rows)
    return jnp.where((jnp.abs(rel) <= half_window)[None, :], rows, NEG_INF)


_A_HEAD_ORDER = tuple(h for p in range(A_Q_HEADS // 2) for h in (p, p + A_Q_HEADS // 2))


def kernel(x, norm_gain, w_in, q_norm_a, k_norm_a, q_norm_b, k_norm_b, sink_a, rel_bias,
           w_branch_a, w_branch_b, b_merge, w_out):
    bsz, seq, d = x.shape
    scale = HEAD_DIM ** -0.5

    expand = jnp.asarray(np.kron(np.eye(LANES, B_HEADS), np.ones((1, HEAD_DIM))), BF16)
    for layer in range(norm_gain.shape[0]):
        w = w_in[layer].astype(BF16)
        wa = jnp.concatenate([w_branch_a[layer][h * HEAD_DIM:(h + 1) * HEAD_DIM] for h in _A_HEAD_ORDER],
                             axis=0).astype(BF16)
        wb = w_branch_b[layer].astype(BF16)
        wo = w_out[layer].astype(BF16)
        gqa = jnp.tile(q_norm_a[layer] * (scale * LOG2E), A_WIDTH // HEAD_DIM)[None]
        gka = jnp.tile(k_norm_a[layer], A_KV_HEADS)[None]
        gqb = jnp.tile(q_norm_b[layer] * (scale * LOG2E), B_HEADS)[None]
        gkb = jnp.tile(k_norm_b[layer], B_HEADS)[None]
        bm = b_merge[layer].reshape(1, -1)
        sink = sink_a[layer][np.asarray(_A_HEAD_ORDER)] * LOG2E

        x2 = x.reshape(bsz * seq, d)
        qa, ka, va, qbs, kbs, vbs, sga, sgb, gates = _in_proj(
            x2, norm_gain[layer][None], w, gqa, gka, gqb, gkb, bm, tm=512)

        obs, stats = [], []
        for g, (window, dil) in enumerate(B_GROUPS):
            c0 = A_Q_HEADS + g * B_HEADS
            hw = window // (2 * dil)
            sub = seq // dil
            bias_g = _bias_rows(rel_bias[:, c0:c0 + B_HEADS], hw, dil)
            view = (bsz, sub, dil * B_WIDTH)
            o_g, st_g = _band_attn(qbs[g].reshape(view), kbs[g].reshape(view), vbs[g].reshape(view), bias_g,
                                   seq_len=sub, tq=min(sub, 2048), half_window=hw,
                                   n_cls=min(dil, 8), unroll=max(2, 16 // dil))
            obs.append(o_g.reshape(bsz * sub, dil * B_WIDTH))
            stats.append(st_g.reshape(bsz * sub, dil * LANES))

        bias_a = _bias_rows(rel_bias[:, :A_Q_HEADS][:, np.asarray(_A_HEAD_ORDER)], A_WINDOW, 1)
        y2 = _merge_out(x2, qa, ka.reshape(bsz, seq, PAIR), va.reshape(bsz, seq, PAIR), bias_a, sink,
                        sga, obs, stats, sgb, gates, expand, wa, wb, wo, tm=512, seq_len=seq)
        x = y2.reshape(bsz, seq, d)
    return x
```

```python
import functools
import math

import jax
import jax.numpy as jnp
import numpy as np
from jax import lax
from jax.experimental import pallas as pl
from jax.experimental.pallas import tpu as pltpu

D_MODEL = 1024
HEAD_DIM = 64
A_Q_HEADS = 8
A_KV_HEADS = 2
A_WINDOW = 128
B_GROUPS = ((128, 1), (512, 4), (2048, 16))
B_HEADS = 8
A_WIDTH = A_Q_HEADS * HEAD_DIM
B_WIDTH = B_HEADS * HEAD_DIM
N_BUCKETS = 32
MAX_DISTANCE = 1024
EPS = 1e-6
NEG_INF = -1e30
LOG2E = 1.0 / math.log(2.0)

LANES = 128
PAIR = 2 * HEAD_DIM
Q_BLK = 128
SEG = 256
VMEM_LIMIT = 56 * 1024 * 1024

BF16 = jnp.bfloat16
F32 = jnp.float32


def _head_norm(y, gain):
    lo = lax.broadcasted_iota(jnp.int32, (1, PAIR), 1) < HEAD_DIM
    parts = []
    for c in range(0, y.shape[1], PAIR):
        yc = y[:, c:c + PAIR]
        sq = yc * yc
        s_lo = jnp.sum(jnp.where(lo, sq, 0.0), axis=-1, keepdims=True)
        s_hi = jnp.sum(jnp.where(lo, 0.0, sq), axis=-1, keepdims=True)
        ms = jnp.where(lo, s_lo, s_hi) * (1.0 / HEAD_DIM)
        parts.append(yc * lax.rsqrt(ms + EPS))
    yn = parts[0] if len(parts) == 1 else jnp.concatenate(parts, axis=1)
    return yn * gain


def _silu(y):
    return y * (1.0 / (1.0 + jnp.exp(-y)))


def _pair_order(y):
    lo = lax.broadcasted_iota(jnp.int32, (1, PAIR), 1) < HEAD_DIM
    n_pairs = A_Q_HEADS // 2
    cols = [y[:, g * PAIR:(g + 1) * PAIR] for g in range(n_pairs)]
    swapped = [pltpu.roll(col, HEAD_DIM, 1) for col in cols]
    out = []
    for p in range(n_pairs):
        first, second = p, p + n_pairs
        lo_src = cols[first // 2] if first % 2 == 0 else swapped[first // 2]
        hi_src = cols[second // 2] if second % 2 == 1 else swapped[second // 2]
        out.append(jnp.where(lo, lo_src, hi_src))
    return jnp.concatenate(out, axis=1)


def _store_by_class(y, dil, scr_ref, out_ref, col0, width):
    rows, w = y.shape
    if dil == 1:
        out_ref[:, col0:col0 + w] = y.astype(out_ref.dtype)
        return
    slabs = w // LANES
    for s in range(slabs):
        scr_ref[s] = y[:, s * LANES:(s + 1) * LANES]
    for c in range(dil):
        for s in range(slabs):
            col = c * width + col0 + s * LANES
            out_ref[:, col:col + LANES] = scr_ref[s, pl.ds(c, rows // dil, stride=dil), :].astype(out_ref.dtype)


def _load_by_class(ref, dil, scr_ref, width):
    if dil == 1:
        return ref[...].astype(F32)
    sub_rows = ref.shape[0]
    slabs = width // LANES
    for c in range(dil):
        for s in range(slabs):
            col = c * width + s * LANES
            scr_ref[s, pl.ds(c, sub_rows, stride=dil), :] = ref[:, col:col + LANES].astype(F32)
    parts = [scr_ref[s] for s in range(slabs)]
    return parts[0] if slabs == 1 else jnp.concatenate(parts, axis=1)


def _in_proj_kernel(x_ref, ng_ref, w_ref, gqa_ref, gka_ref, gqb_ref, gkb_ref, bm_ref,
                    qa_ref, ka_ref, va_ref, qb_refs, kb_refs, vb_refs, sga_ref, sgb_ref, gate_ref,
                    scr_refs):
    x = x_ref[...]
    ms = jnp.mean(x * x, axis=-1, keepdims=True)
    h = ((x * lax.rsqrt(ms + EPS)) * ng_ref[...]).astype(BF16)

    def segments(c0, width):
        for off in range(0, width, SEG):
            yield off, jnp.dot(h, w_ref[:, c0 + off:c0 + off + SEG], preferred_element_type=F32)

    n_g = len(B_GROUPS)
    c_kva = A_WIDTH
    c_b = c_kva + 2 * PAIR
    c_ga = c_b + 3 * n_g * B_WIDTH
    c_gb = c_ga + A_WIDTH
    c_mg = c_gb + B_WIDTH

    for off, y in segments(c_mg, 2 * D_MODEL):
        y = y + bm_ref[:, off:off + SEG]
        gate_ref[:, off:off + SEG] = (1.0 / (1.0 + jnp.exp(-y))).astype(BF16)
    sga = [_silu(y) for _, y in segments(c_ga, A_WIDTH)]
    sga_ref[...] = _pair_order(jnp.concatenate(sga, axis=1)).astype(BF16)
    for off, y in segments(c_gb, B_WIDTH):
        sgb_ref[:, off:off + SEG] = _silu(y).astype(BF16)
    qa = [_head_norm(y, gqa_ref[:, off:off + SEG]) for off, y in segments(0, A_WIDTH)]
    qa_ref[...] = _pair_order(jnp.concatenate(qa, axis=1)).astype(BF16)
    n_scr = 0
    for part, (refs, gain_ref) in enumerate(((qb_refs, gqb_ref), (kb_refs, gkb_ref), (vb_refs, None))):
        for g, (_, dil) in enumerate(B_GROUPS):
            for off, y in segments(c_b + (part * n_g + g) * B_WIDTH, B_WIDTH):
                if gain_ref is not None:
                    y = _head_norm(y, gain_ref[:, off:off + SEG])
                _store_by_class(y, dil, scr_refs[n_scr % len(scr_refs)], refs[g], off, B_WIDTH)
                n_scr += dil > 1
    for _, y in segments(c_kva, 2 * PAIR):
        ka_ref[...] = _head_norm(y[:, :PAIR], gka_ref[...]).astype(BF16)
        va_ref[...] = y[:, PAIR:].astype(BF16)


def _in_proj(x2, ng, w, gqa, gka, gqb, gkb, bm, tm):
    m = x2.shape[0]

    def rows(width):
        return pl.BlockSpec((tm, width), lambda i: (i, 0))

    def whole(a):
        return pl.BlockSpec(a.shape, lambda i: (0,) * a.ndim)

    def out(width):
        return jax.ShapeDtypeStruct((m, width), BF16)

    b_shape = [jax.ShapeDtypeStruct((m // dil, dil * B_WIDTH), BF16) for _, dil in B_GROUPS]
    b_spec = [pl.BlockSpec((tm // dil, dil * B_WIDTH), lambda i: (i, 0)) for _, dil in B_GROUPS]
    out_shape = (out(A_WIDTH), out(PAIR), out(PAIR), b_shape, b_shape, b_shape,
                 out(A_WIDTH), out(B_WIDTH), out(2 * D_MODEL))
    out_specs = (rows(A_WIDTH), rows(PAIR), rows(PAIR), b_spec, b_spec, b_spec,
                 rows(A_WIDTH), rows(B_WIDTH), rows(2 * D_MODEL))
    scratch = [[pltpu.VMEM((SEG // LANES, tm, LANES), F32)] * 2]
    return pl.pallas_call(
        _in_proj_kernel,
        grid=(m // tm,),
        in_specs=[rows(D_MODEL), whole(ng), pl.BlockSpec(memory_space=pltpu.VMEM),
                  whole(gqa), whole(gka), whole(gqb), whole(gkb), whole(bm)],
        out_specs=out_specs,
        out_shape=out_shape,
        scratch_shapes=scratch,
        compiler_params=pltpu.CompilerParams(vmem_limit_bytes=VMEM_LIMIT),
        name="in_proj",
    )(x2, ng, w, gqa, gka, gqb, gkb, bm)


def _init_bias_tiles(bias_row_ref, bias_ref, *, half_window, n_heads, stack):
    tk = Q_BLK + 2 * half_window
    width = bias_row_ref.shape[1]
    for li, lead in enumerate((0, half_window, 2 * half_window)):
        shift = width - (Q_BLK - 1 + 2 * half_window - lead)
        for head in range(n_heads):
            base = jnp.broadcast_to(bias_row_ref[head:head + 1, :], (Q_BLK, width))
            tile = pltpu.roll(base, shift, 1, stride=1, stride_axis=0)
            grp, idx = divmod(head, 2 * stack)
            bias_ref[li, grp, idx * Q_BLK:(idx + 1) * Q_BLK, :] = tile[:, :tk]


def _attn_block(blk, load_q, load_kv, bias_ref, sink_ref, store_out, store_stats, *,
                seq_len, half_window, n_cls, q_pairs, kv_pairs, stack):
    tk = Q_BLK + 2 * half_window
    n_blk = seq_len // Q_BLK
    lo = lax.broadcasted_iota(jnp.int32, (1, PAIR), 1) < HEAD_DIM
    lane_l = lax.broadcasted_iota(jnp.int32, (1, LANES), 1)
    ws = pl.multiple_of(jnp.clip(blk * Q_BLK - half_window, 0, seq_len - tk), HEAD_DIM)
    sel = (blk > 0).astype(jnp.int32) + (blk == n_blk - 1).astype(jnp.int32)
    ones = jnp.ones((tk, PAIR), BF16)
    for c in range(n_cls):
        st_tile = jnp.where(lane_l < HEAD_DIM, jnp.zeros((Q_BLK, LANES), F32), 1.0)
        for grp in range(q_pairs // stack):
            k, v = load_kv(ws, (c * kv_pairs + grp * stack * kv_pairs // q_pairs) * PAIR)
            qcols = [(c * q_pairs + grp * stack + i) * PAIR for i in range(stack)]
            qs = []
            for qcol in qcols:
                q = load_q(qcol)
                zero = jnp.zeros_like(q)
                qs += [jnp.where(lo, q, zero), jnp.where(lo, zero, q)]
            s = lax.dot_general(jnp.concatenate(qs, axis=0), k, (((1,), (1,)), ((), ())),
                                preferred_element_type=F32)
            s = s + bias_ref[sel, grp]
            heads = range(2 * stack * grp, 2 * stack * (grp + 1))
            ms, es = [], []
            for i, head in enumerate(heads):
                s_h = s[i * Q_BLK:(i + 1) * Q_BLK]
                m_h = jnp.max(s_h, axis=-1, keepdims=True)
                if sink_ref is not None:
                    m_h = jnp.maximum(m_h, sink_ref[head])
                ms.append(m_h)
                es.append(jnp.exp2(s_h - m_h).astype(BF16))
            pv = jnp.dot(jnp.concatenate(es, axis=0), jnp.concatenate([v, ones], axis=1),
                         preferred_element_type=F32)
            outs = []
            for i, head in enumerate(heads):
                l_h = pv[i * Q_BLK:(i + 1) * Q_BLK, PAIR:]
                if sink_ref is not None:
                    l_h = l_h + jnp.exp2(sink_ref[head] - ms[i])
                if store_stats is not None:
                    outs.append(pv[i * Q_BLK:(i + 1) * Q_BLK, :PAIR])
                    st_tile = jnp.where(lane_l == head, ms[i], st_tile)
                    st_tile = jnp.where(lane_l == HEAD_DIM + head, l_h, st_tile)
                else:
                    outs.append(pv[i * Q_BLK:(i + 1) * Q_BLK, :PAIR] * (1.0 / l_h))
            for i, qcol in enumerate(qcols):
                store_out(qcol, jnp.where(lo, outs[2 * i], outs[2 * i + 1]))
        if store_stats is not None:
            store_stats(c, st_tile)


def _band_attn_kernel(q_ref, k_ref, v_ref, bias_row_ref, o_ref, st_ref, bias_ref, *,
                      seq_len, tq, half_window, n_cls, q_pairs, unroll):
    t = pl.program_id(2)

    @pl.when((pl.program_id(0) == 0) & (pl.program_id(1) == 0) & (t == 0))
    def _():
        _init_bias_tiles(bias_row_ref, bias_ref, half_window=half_window, n_heads=2 * q_pairs, stack=1)

    tk = Q_BLK + 2 * half_window

    def block(j, carry):
        row = pl.multiple_of(j * Q_BLK, Q_BLK)

        def store_out(col, x):
            o_ref[0, pl.ds(row, Q_BLK), col:col + PAIR] = x.astype(BF16)

        def store_stats(c, tile):
            st_ref[0, pl.ds(row, Q_BLK), c * LANES:(c + 1) * LANES] = tile

        _attn_block(t * (tq // Q_BLK) + j,
                    lambda col: q_ref[0, pl.ds(row, Q_BLK), col:col + PAIR],
                    lambda ws, col: (k_ref[0, pl.ds(ws, tk), col:col + PAIR],
                                     v_ref[0, pl.ds(ws, tk), col:col + PAIR]),
                    bias_ref, None, store_out, store_stats, seq_len=seq_len, half_window=half_window,
                    n_cls=n_cls, q_pairs=q_pairs, kv_pairs=q_pairs, stack=1)
        return carry

    lax.fori_loop(0, tq // Q_BLK, block, 0, unroll=unroll)


def _band_attn(q, k, v, bias_rows, *, seq_len, tq, half_window, n_cls, unroll):
    n_rows, _, cols = q.shape
    tk = Q_BLK + 2 * half_window
    q_pairs = bias_rows.shape[0] // 2
    n_col = cols // (q_pairs * PAIR)
    assert k.shape == q.shape and v.shape == q.shape and bias_rows.shape[1] % LANES == 0
    assert seq_len % tq == 0 and tq % Q_BLK == 0 and n_col % n_cls == 0

    blk = n_cls * q_pairs * PAIR
    rows_spec = pl.BlockSpec((1, tq, blk), lambda r, c, t: (r, t, c))
    seq_spec = pl.BlockSpec((1, seq_len, blk), lambda r, c, t: (r, 0, c))
    kern = functools.partial(_band_attn_kernel, seq_len=seq_len, tq=tq, half_window=half_window,
                             n_cls=n_cls, q_pairs=q_pairs, unroll=unroll)
    return pl.pallas_call(
        kern,
        grid=(n_rows, n_col // n_cls, seq_len // tq),
        in_specs=[rows_spec, seq_spec, seq_spec, pl.BlockSpec(bias_rows.shape, lambda r, c, t: (0, 0))],
        out_specs=[rows_spec, pl.BlockSpec((1, tq, n_cls * LANES), lambda r, c, t: (r, t, c))],
        out_shape=[jax.ShapeDtypeStruct(q.shape, BF16),
                   jax.ShapeDtypeStruct((n_rows, seq_len, n_col * LANES), F32)],
        scratch_shapes=[pltpu.VMEM((3, q_pairs, 2 * Q_BLK, tk), F32)],
        compiler_params=pltpu.CompilerParams(vmem_limit_bytes=VMEM_LIMIT,
                                             dimension_semantics=("arbitrary",) * 3),
        name=f"band_attn_len{seq_len}",
    )(q, k, v, bias_rows)


def _merge_out_kernel(x_ref, qa_ref, ka_ref, va_ref, bias_row_ref, sink_ref, sga_ref, ob_refs, st_refs,
                      sgb_ref, gate_ref, expand_ref, wa_ref, wb_ref, wo_ref, y_ref,
                      bias_ref, ya_ref, scr_o_refs, scr_l_refs, *, seq_len):
    i = pl.program_id(0)
    tm = x_ref.shape[0]
    stack = A_Q_HEADS // 2

    @pl.when(i == 0)
    def _():
        _init_bias_tiles(bias_row_ref, bias_ref, half_window=A_WINDOW, n_heads=A_Q_HEADS, stack=stack)

    first_blk = lax.rem(i, seq_len // tm) * (tm // Q_BLK)
    tk = Q_BLK + 2 * A_WINDOW
    for j in range(tm // Q_BLK):
        rows = slice(j * Q_BLK, (j + 1) * Q_BLK)

        def store_out(col, x, rows=rows):
            ya_ref[rows, col:col + PAIR] = (x * sga_ref[rows, col:col + PAIR].astype(F32)).astype(BF16)

        _attn_block(first_blk + j,
                    lambda col, rows=rows: qa_ref[rows, col:col + PAIR],
                    lambda ws, col: (ka_ref[0, pl.ds(ws, tk), col:col + PAIR],
                                     va_ref[0, pl.ds(ws, tk), col:col + PAIR]),
                    bias_ref, sink_ref, store_out, None, seq_len=seq_len, half_window=A_WINDOW, n_cls=1,
                    q_pairs=stack, kv_pairs=A_KV_HEADS // 2, stack=stack)

    dils = [dil for _, dil in B_GROUPS]
    stats = [_load_by_class(r, dil, s, LANES) for r, dil, s in zip(st_refs, dils, scr_l_refs)]
    is_head = lax.broadcasted_iota(jnp.int32, (1, LANES), 1) < HEAD_DIM
    m = functools.reduce(jnp.maximum, stats)
    es = [jnp.exp2(st - m) for st in stats]
    den = functools.reduce(lambda a, b: a + b, [e * pltpu.roll(st, HEAD_DIM, 1) for e, st in zip(es, stats)])
    inv = jnp.where(is_head, 1.0 / den, 0.0)
    yb = None
    for e, o_ref, dil, scr in zip(es, ob_refs, dils, scr_o_refs):
        alpha = jnp.dot((e * inv).astype(BF16), expand_ref[...], preferred_element_type=F32)
        term = alpha * _load_by_class(o_ref, dil, scr, B_WIDTH)
        yb = term if yb is None else yb + term
    yb = (yb * sgb_ref[...].astype(F32)).astype(BF16)
    br_a = jnp.dot(ya_ref[...], wa_ref[...], preferred_element_type=F32)
    br_b = jnp.dot(yb, wb_ref[...], preferred_element_type=F32)
    merged = (gate_ref[:, :D_MODEL].astype(F32) * br_a
              + gate_ref[:, D_MODEL:].astype(F32) * br_b).astype(BF16)
    y_ref[...] = x_ref[...] + jnp.dot(merged, wo_ref[...], preferred_element_type=F32)


def _merge_out(x2, qa, ka, va, bias_rows, sink, sga, obs, stats, sgb, gates, expand, wa, wb, wo, *, tm, seq_len):
    m = x2.shape[0]
    tiles_per_seq = seq_len // tm

    def rows(width):
        return pl.BlockSpec((tm, width), lambda i: (i, 0))

    def whole(a):
        return pl.BlockSpec(a.shape, lambda i: (0,) * a.ndim)

    def by_class(width):
        return [pl.BlockSpec((tm // dil, dil * width), lambda i: (i, 0)) for _, dil in B_GROUPS]

    seq_kv = pl.BlockSpec((1, seq_len, PAIR), lambda i: (i // tiles_per_seq, 0, 0))
    n_g = len(B_GROUPS)
    stack = A_Q_HEADS // 2
    scratch = [pltpu.VMEM((3, 1, 2 * stack * Q_BLK, Q_BLK + 2 * A_WINDOW), F32),
               pltpu.VMEM((tm, A_WIDTH), BF16),
               [pltpu.VMEM((B_WIDTH // LANES, tm, LANES), F32)] * n_g,
               [pltpu.VMEM((1, tm, LANES), F32)] * n_g]
    return pl.pallas_call(
        functools.partial(_merge_out_kernel, seq_len=seq_len),
        grid=(m // tm,),
        in_specs=[rows(D_MODEL), rows(A_WIDTH), seq_kv, seq_kv, whole(bias_rows),
                  pl.BlockSpec(memory_space=pltpu.SMEM), rows(A_WIDTH), by_class(B_WIDTH),
                  by_class(LANES), rows(B_WIDTH), rows(2 * D_MODEL),
                  whole(expand), whole(wa), whole(wb), whole(wo)],
        out_specs=rows(D_MODEL),
        out_shape=jax.ShapeDtypeStruct((m, D_MODEL), F32),
        scratch_shapes=scratch,
        compiler_params=pltpu.CompilerParams(vmem_limit_bytes=VMEM_LIMIT, dimension_semantics=("arbitrary",)),
        name="merge_out",
    )(x2, qa, ka, va, bias_rows, sink.astype(F32), sga, obs, stats, sgb, gates, expand, wa, wb, wo)


def _t5_bucket(rel):
    half = N_BUCKETS // 2
    max_exact = half // 2
    ret = (rel > 0).astype(jnp.int32) * half
    n = jnp.abs(rel)
    nf = jnp.maximum(n, max_exact).astype(jnp.float32)
    large = max_exact + jnp.floor(jnp.log(nf / max_exact) / math.log(MAX_DISTANCE / max_exact)
                                  * (half - max_exact)).astype(jnp.int32)
    large = jnp.minimum(large, half - 1)
    return ret + jnp.where(n < max_exact, n, large)


def _bias_rows(table, half_window, stride):
    n_rel = 2 * Q_BLK - 1 + 4 * half_window
    width = -(-n_rel // LANES) * LANES
    rel = jnp.arange(width) - (Q_BLK - 1 + 2 * half_window)
    bucket = _t5_bucket(rel * stride)[None, :]
    tab = table.astype(F32) * LOG2E
    rows = jnp.zeros((table.shape[1], width), F32)
    for b in range(N_BUCKETS):
        rows = jnp.where(bucket == b, tab[b][:, None], rows)
    return jnp.where((jnp.abs(rel) <= half_window)[None, :], rows, NEG_INF)


_A_HEAD_ORDER = tuple(h for p in range(A_Q_HEADS // 2) for h in (p, p + A_Q_HEADS // 2))


def kernel(x, norm_gain, w_in, q_norm_a, k_norm_a, q_norm_b, k_norm_b, sink_a, rel_bias,
           w_branch_a, w_branch_b, b_merge, w_out):
    bsz, seq, d = x.shape
    scale = HEAD_DIM ** -0.5

    expand = jnp.asarray(np.kron(np.eye(LANES, B_HEADS), np.ones((1, HEAD_DIM))), BF16)
    for layer in range(norm_gain.shape[0]):
        w = w_in[layer].astype(BF16)
        wa = jnp.concatenate([w_branch_a[layer][h * HEAD_DIM:(h + 1) * HEAD_DIM] for h in _A_HEAD_ORDER],
                             axis=0).astype(BF16)
        wb = w_branch_b[layer].astype(BF16)
        wo = w_out[layer].astype(BF16)
        gqa = jnp.tile(q_norm_a[layer] * (scale * LOG2E), A_WIDTH // HEAD_DIM)[None]
        gka = jnp.tile(k_norm_a[layer], A_KV_HEADS)[None]
        gqb = jnp.tile(q_norm_b[layer] * (scale * LOG2E), B_HEADS)[None]
        gkb = jnp.tile(k_norm_b[layer], B_HEADS)[None]
        bm = b_merge[layer].reshape(1, -1)
        sink = sink_a[layer][np.asarray(_A_HEAD_ORDER)] * LOG2E

        x2 = x.reshape(bsz * seq, d)
        qa, ka, va, qbs, kbs, vbs, sga, sgb, gates = _in_proj(
            x2, norm_gain[layer][None], w, gqa, gka, gqb, gkb, bm, tm=512)

        obs, stats = [], []
        for g, (window, dil) in enumerate(B_GROUPS):
            c0 = A_Q_HEADS + g * B_HEADS
            hw = window // (2 * dil)
            sub = seq // dil
            bias_g = _bias_rows(rel_bias[:, c0:c0 + B_HEADS], hw, dil)
            view = (bsz, sub, dil * B_WIDTH)
            o_g, st_g = _band_attn(qbs[g].reshape(view), kbs[g].reshape(view), vbs[g].reshape(view), bias_g,
                                   seq_len=sub, tq=min(sub, 2048), half_window=hw,
                                   n_cls=min(dil, 8), unroll=max(2, 16 // dil))
            obs.append(o_g.reshape(bsz * sub, dil * B_WIDTH))
            stats.append(st_g.reshape(bsz * sub, dil * LANES))

        bias_a = _bias_rows(rel_bias[:, :A_Q_HEADS][:, np.asarray(_A_HEAD_ORDER)], A_WINDOW, 1)
        y2 = _merge_out(x2, qa, ka.reshape(bsz, seq, PAIR), va.reshape(bsz, seq, PAIR), bias_a, sink,
                        sga, obs, stats, sgb, gates, expand, wa, wb, wo, tm=512, seq_len=seq)
        x = y2.reshape(bsz, seq, d)
    return x
```

```python
import functools
import math

import jax
import jax.numpy as jnp
import numpy as np
from jax import lax
from jax.experimental import pallas as pl
from jax.experimental.pallas import tpu as pltpu

D_MODEL = 1024
HEAD_DIM = 64
A_Q_HEADS = 8
A_KV_HEADS = 2
A_WINDOW = 128
B_GROUPS = ((128, 1), (512, 4), (2048, 16))
B_HEADS = 8
A_WIDTH = A_Q_HEADS * HEAD_DIM
B_WIDTH = B_HEADS * HEAD_DIM
N_BUCKETS = 32
MAX_DISTANCE = 1024
EPS = 1e-6
NEG_INF = -1e30
LOG2E = 1.0 / math.log(2.0)

LANES = 128
PAIR = 2 * HEAD_DIM
Q_BLK = 128
SEG = 256
B_ROWS_PER_STEP = 2048
B_UNITS_PER_BODY = 128
VMEM_LIMIT = 56 * 1024 * 1024

BF16 = jnp.bfloat16
F32 = jnp.float32


def _head_norm(y, gain):
    lo = lax.broadcasted_iota(jnp.int32, (1, PAIR), 1) < HEAD_DIM
    parts = []
    for c in range(0, y.shape[1], PAIR):
        yc = y[:, c:c + PAIR]
        sq = yc * yc
        s_lo = jnp.sum(jnp.where(lo, sq, 0.0), axis=-1, keepdims=True)
        s_hi = jnp.sum(jnp.where(lo, 0.0, sq), axis=-1, keepdims=True)
        ms = jnp.where(lo, s_lo, s_hi) * (1.0 / HEAD_DIM)
        parts.append(yc * lax.rsqrt(ms + EPS))
    yn = parts[0] if len(parts) == 1 else jnp.concatenate(parts, axis=1)
    return yn * gain


def _silu(y):
    return y * (1.0 / (1.0 + jnp.exp(-y)))


def _pair_order(y):
    lo = lax.broadcasted_iota(jnp.int32, (1, PAIR), 1) < HEAD_DIM
    n_pairs = A_Q_HEADS // 2
    cols = [y[:, g * PAIR:(g + 1) * PAIR] for g in range(n_pairs)]
    swapped = [pltpu.roll(col, HEAD_DIM, 1) for col in cols]
    out = []
    for p in range(n_pairs):
        first, second = p, p + n_pairs
        lo_src = cols[first // 2] if first % 2 == 0 else swapped[first // 2]
        hi_src = cols[second // 2] if second % 2 == 1 else swapped[second // 2]
        out.append(jnp.where(lo, lo_src, hi_src))
    return jnp.concatenate(out, axis=1)


def _store_by_class(y, dil, scr_ref, out_ref, col0, width):
    rows, w = y.shape
    if dil == 1:
        out_ref[:, col0:col0 + w] = y.astype(out_ref.dtype)
        return
    slabs = w // LANES
    for s in range(slabs):
        scr_ref[s] = y[:, s * LANES:(s + 1) * LANES]
    for c in range(dil):
        for s in range(slabs):
            col = c * width + col0 + s * LANES
            out_ref[:, col:col + LANES] = scr_ref[s, pl.ds(c, rows // dil, stride=dil), :].astype(out_ref.dtype)


def _load_by_class(ref, dil, scr_ref, width):
    if dil == 1:
        return ref[...].astype(F32)
    sub_rows = ref.shape[0]
    slabs = width // LANES
    for c in range(dil):
        for s in range(slabs):
            col = c * width + s * LANES
            scr_ref[s, pl.ds(c, sub_rows, stride=dil), :] = ref[:, col:col + LANES].astype(F32)
    parts = [scr_ref[s] for s in range(slabs)]
    return parts[0] if slabs == 1 else jnp.concatenate(parts, axis=1)


def _in_proj_kernel(x_ref, ng_ref, w_ref, gqa_ref, gka_ref, gqb_ref, gkb_ref, bm_ref,
                    qa_ref, ka_ref, va_ref, qb_refs, kb_refs, vb_refs, sga_ref, sgb_ref, gate_ref,
                    scr_refs):
    x = x_ref[...]
    ms = jnp.mean(x * x, axis=-1, keepdims=True)
    h = ((x * lax.rsqrt(ms + EPS)) * ng_ref[...]).astype(BF16)

    def proj(c0):
        return jnp.dot(h, w_ref[:, c0:c0 + SEG], preferred_element_type=F32)

    n_g = len(B_GROUPS)
    c_kva = A_WIDTH
    c_b = c_kva + 2 * PAIR
    c_ga = c_b + 3 * n_g * B_WIDTH
    c_gb = c_ga + A_WIDTH
    c_mg = c_gb + B_WIDTH
    offs = range(0, B_WIDTH, SEG)

    def gate_job(off):
        def job():
            y = proj(c_mg + off) + bm_ref[:, off:off + SEG]
            gate_ref[:, off:off + SEG] = (1.0 / (1.0 + jnp.exp(-y))).astype(BF16)
        return job

    def sgb_job(off):
        def job():
            sgb_ref[:, off:off + SEG] = _silu(proj(c_gb + off)).astype(BF16)
        return job

    def sga_job():
        sga = [_silu(proj(c_ga + off)) for off in range(0, A_WIDTH, SEG)]
        sga_ref[...] = _pair_order(jnp.concatenate(sga, axis=1)).astype(BF16)

    def qa_job():
        qa = [_head_norm(proj(off), gqa_ref[:, off:off + SEG]) for off in range(0, A_WIDTH, SEG)]
        qa_ref[...] = _pair_order(jnp.concatenate(qa, axis=1)).astype(BF16)

    def kva_job():
        y = proj(c_kva)
        ka_ref[...] = _head_norm(y[:, :PAIR], gka_ref[...]).astype(BF16)
        va_ref[...] = y[:, PAIR:].astype(BF16)

    n_scr = [0]

    def group_job(part, refs, gain_ref, g, off):
        def job():
            dil = B_GROUPS[g][1]
            y = proj(c_b + (part * n_g + g) * B_WIDTH + off)
            if gain_ref is not None:
                y = _head_norm(y, gain_ref[:, off:off + SEG])
            _store_by_class(y, dil, scr_refs[n_scr[0] % len(scr_refs)], refs[g], off, B_WIDTH)
            n_scr[0] += dil > 1
        return job

    exp_jobs = [gate_job(off) for off in range(0, 2 * D_MODEL, SEG)] + [sga_job] + [sgb_job(off) for off in offs]
    norm_jobs = [group_job(p, refs, gain, g, off) for p, (refs, gain) in enumerate(((qb_refs, gqb_ref), (kb_refs, gkb_ref)))
                 for g in range(n_g) for off in offs] + [qa_job]
    plain_jobs = [group_job(2, vb_refs, None, g, off) for g in range(n_g) for off in offs] + [kva_job]
    order = []
    for i in range(max(len(exp_jobs), len(norm_jobs))):
        order += exp_jobs[i:i + 1] + norm_jobs[i:i + 1]
    for job in order + plain_jobs:
        job()


def _in_proj(x2, ng, w, gqa, gka, gqb, gkb, bm, tm):
    m = x2.shape[0]

    def rows(width):
        return pl.BlockSpec((tm, width), lambda i: (i, 0))

    def whole(a):
        return pl.BlockSpec(a.shape, lambda i: (0,) * a.ndim)

    def out(width):
        return jax.ShapeDtypeStruct((m, width), BF16)

    b_shape = [jax.ShapeDtypeStruct((m // dil, dil * B_WIDTH), BF16) for _, dil in B_GROUPS]
    b_spec = [pl.BlockSpec((tm // dil, dil * B_WIDTH), lambda i: (i, 0)) for _, dil in B_GROUPS]
    out_shape = (out(A_WIDTH), out(PAIR), out(PAIR), b_shape, b_shape, b_shape,
                 out(A_WIDTH), out(B_WIDTH), out(2 * D_MODEL))
    out_specs = (rows(A_WIDTH), rows(PAIR), rows(PAIR), b_spec, b_spec, b_spec,
                 rows(A_WIDTH), rows(B_WIDTH), rows(2 * D_MODEL))
    scratch = [[pltpu.VMEM((SEG // LANES, tm, LANES), F32)] * 2]
    return pl.pallas_call(
        _in_proj_kernel,
        grid=(m // tm,),
        in_specs=[rows(D_MODEL), whole(ng), pl.BlockSpec(memory_space=pltpu.VMEM),
                  whole(gqa), whole(gka), whole(gqb), whole(gkb), whole(bm)],
        out_specs=out_specs,
        out_shape=out_shape,
        scratch_shapes=scratch,
        compiler_params=pltpu.CompilerParams(vmem_limit_bytes=VMEM_LIMIT),
        name="in_proj",
    )(x2, ng, w, gqa, gka, gqb, gkb, bm)


def _init_bias_tiles(bias_row_ref, bias_ref, *, half_window, n_heads, stack):
    tk = Q_BLK + 2 * half_window
    width = bias_row_ref.shape[1]
    for li, lead in enumerate((0, half_window, 2 * half_window)):
        shift = width - (Q_BLK - 1 + 2 * half_window - lead)
        for head in range(n_heads):
            base = jnp.broadcast_to(bias_row_ref[head:head + 1, :], (Q_BLK, width))
            tile = pltpu.roll(base, shift, 1, stride=1, stride_axis=0)
            grp, idx = divmod(head, 2 * stack)
            bias_ref[li, grp, idx * Q_BLK:(idx + 1) * Q_BLK, :] = tile[:, :tk]


def _attn_block(blk, load_q, load_kv, bias_ref, sink_ref, store_out, store_stats, *,
                seq_len, half_window, n_cls, q_pairs, kv_pairs, stack):
    tk = Q_BLK + 2 * half_window
    n_blk = seq_len // Q_BLK
    lo = lax.broadcasted_iota(jnp.int32, (1, PAIR), 1) < HEAD_DIM
    lane_l = lax.broadcasted_iota(jnp.int32, (1, LANES), 1)
    ws = pl.multiple_of(jnp.clip(blk * Q_BLK - half_window, 0, seq_len - tk), HEAD_DIM)
    sel = (blk > 0).astype(jnp.int32) + (blk == n_blk - 1).astype(jnp.int32)
    ones = jnp.ones((tk, PAIR), BF16)
    for c in range(n_cls):
        st_tile = jnp.where(lane_l < HEAD_DIM, jnp.zeros((Q_BLK, LANES), F32), 1.0)
        for grp in range(q_pairs // stack):
            k, v = load_kv(ws, (c * kv_pairs + grp * stack * kv_pairs // q_pairs) * PAIR)
            qcols = [(c * q_pairs + grp * stack + i) * PAIR for i in range(stack)]
            qs = []
            for qcol in qcols:
                q = load_q(qcol)
                zero = jnp.zeros_like(q)
                qs += [jnp.where(lo, q, zero), jnp.where(lo, zero, q)]
            s = lax.dot_general(jnp.concatenate(qs, axis=0), k, (((1,), (1,)), ((), ())),
                                preferred_element_type=F32)
            s = s + bias_ref[sel, grp]
            heads = range(2 * stack * grp, 2 * stack * (grp + 1))
            ms, es = [], []
            for i, head in enumerate(heads):
                s_h = s[i * Q_BLK:(i + 1) * Q_BLK]
                m_h = jnp.max(s_h, axis=-1, keepdims=True)
                if sink_ref is not None:
                    m_h = jnp.maximum(m_h, sink_ref[head])
                ms.append(m_h)
                es.append(jnp.exp2(s_h - m_h).astype(BF16))
            pv = jnp.dot(jnp.concatenate(es, axis=0), jnp.concatenate([v, ones], axis=1),
                         preferred_element_type=F32)
            outs = []
            for i, head in enumerate(heads):
                l_h = pv[i * Q_BLK:(i + 1) * Q_BLK, PAIR:]
                if sink_ref is not None:
                    l_h = l_h + jnp.exp2(sink_ref[head] - ms[i])
                if store_stats is not None:
                    outs.append(pv[i * Q_BLK:(i + 1) * Q_BLK, :PAIR])
                    st_tile = jnp.where(lane_l == head, ms[i], st_tile)
                    st_tile = jnp.where(lane_l == HEAD_DIM + head, l_h, st_tile)
                else:
                    outs.append(pv[i * Q_BLK:(i + 1) * Q_BLK, :PAIR] * (1.0 / l_h))
            for i, qcol in enumerate(qcols):
                store_out(qcol, jnp.where(lo, outs[2 * i], outs[2 * i + 1]))
        if store_stats is not None:
            store_stats(c, st_tile)


def _band_attn_kernel(q_ref, k_ref, v_ref, bias_row_ref, o_ref, st_ref, bias_ref, *,
                      seq_len, tq, half_window, n_cls, q_pairs, unroll):
    t = pl.program_id(2)

    @pl.when((pl.program_id(0) == 0) & (pl.program_id(1) == 0) & (t == 0))
    def _():
        _init_bias_tiles(bias_row_ref, bias_ref, half_window=half_window, n_heads=2 * q_pairs, stack=1)

    tk = Q_BLK + 2 * half_window

    def block(j, carry):
        row = pl.multiple_of(j * Q_BLK, Q_BLK)

        def store_out(col, x):
            o_ref[0, pl.ds(row, Q_BLK), col:col + PAIR] = x.astype(BF16)

        def store_stats(c, tile):
            st_ref[0, pl.ds(row, Q_BLK), c * LANES:(c + 1) * LANES] = tile

        _attn_block(t * (tq // Q_BLK) + j,
                    lambda col: q_ref[0, pl.ds(row, Q_BLK), col:col + PAIR],
                    lambda ws, col: (k_ref[0, pl.ds(ws, tk), col:col + PAIR],
                                     v_ref[0, pl.ds(ws, tk), col:col + PAIR]),
                    bias_ref, None, store_out, store_stats, seq_len=seq_len, half_window=half_window,
                    n_cls=n_cls, q_pairs=q_pairs, kv_pairs=q_pairs, stack=1)
        return carry

    lax.fori_loop(0, tq // Q_BLK, block, 0, unroll=unroll)


def _band_attn(q, k, v, bias_rows, *, seq_len, tq, half_window, n_cls, unroll):
    n_rows, _, cols = q.shape
    tk = Q_BLK + 2 * half_window
    q_pairs = bias_rows.shape[0] // 2
    n_col = cols // (q_pairs * PAIR)
    assert k.shape == q.shape and v.shape == q.shape and bias_rows.shape[1] % LANES == 0
    assert seq_len % tq == 0 and tq % Q_BLK == 0 and n_col % n_cls == 0

    blk = n_cls * q_pairs * PAIR
    rows_spec = pl.BlockSpec((1, tq, blk), lambda r, c, t: (r, t, c))
    seq_spec = pl.BlockSpec((1, seq_len, blk), lambda r, c, t: (r, 0, c))
    kern = functools.partial(_band_attn_kernel, seq_len=seq_len, tq=tq, half_window=half_window,
                             n_cls=n_cls, q_pairs=q_pairs, unroll=unroll)
    return pl.pallas_call(
        kern,
        grid=(n_rows, n_col // n_cls, seq_len // tq),
        in_specs=[rows_spec, seq_spec, seq_spec, pl.BlockSpec(bias_rows.shape, lambda r, c, t: (0, 0))],
        out_specs=[rows_spec, pl.BlockSpec((1, tq, n_cls * LANES), lambda r, c, t: (r, t, c))],
        out_shape=[jax.ShapeDtypeStruct(q.shape, BF16),
                   jax.ShapeDtypeStruct((n_rows, seq_len, n_col * LANES), F32)],
        scratch_shapes=[pltpu.VMEM((3, q_pairs, 2 * Q_BLK, tk), F32)],
        compiler_params=pltpu.CompilerParams(vmem_limit_bytes=VMEM_LIMIT,
                                             dimension_semantics=("arbitrary",) * 3),
        name=f"band_attn_len{seq_len}",
    )(q, k, v, bias_rows)


def _merge_out_kernel(x_ref, qa_ref, ka_ref, va_ref, bias_row_ref, sink_ref, sga_ref, ob_refs, st_refs,
                      sgb_ref, gate_ref, expand_ref, wa_ref, wb_ref, wo_ref, y_ref,
                      bias_ref, ya_ref, scr_o_refs, scr_l_refs, *, seq_len):
    i = pl.program_id(0)
    tm = x_ref.shape[0]
    stack = A_Q_HEADS // 2

    @pl.when(i == 0)
    def _():
        _init_bias_tiles(bias_row_ref, bias_ref, half_window=A_WINDOW, n_heads=A_Q_HEADS, stack=stack)

    first_blk = lax.rem(i, seq_len // tm) * (tm // Q_BLK)
    tk = Q_BLK + 2 * A_WINDOW
    for j in range(tm // Q_BLK):
        rows = slice(j * Q_BLK, (j + 1) * Q_BLK)

        def store_out(col, x, rows=rows):
            ya_ref[rows, col:col + PAIR] = (x * sga_ref[rows, col:col + PAIR].astype(F32)).astype(BF16)

        _attn_block(first_blk + j,
                    lambda col, rows=rows: qa_ref[rows, col:col + PAIR],
                    lambda ws, col: (ka_ref[0, pl.ds(ws, tk), col:col + PAIR],
                                     va_ref[0, pl.ds(ws, tk), col:col + PAIR]),
                    bias_ref, sink_ref, store_out, None, seq_len=seq_len, half_window=A_WINDOW, n_cls=1,
                    q_pairs=stack, kv_pairs=A_KV_HEADS // 2, stack=stack)

    dils = [dil for _, dil in B_GROUPS]
    stats = [_load_by_class(r, dil, s, LANES) for r, dil, s in zip(st_refs, dils, scr_l_refs)]
    is_head = lax.broadcasted_iota(jnp.int32, (1, LANES), 1) < HEAD_DIM
    m = functools.reduce(jnp.maximum, stats)
    es = [jnp.exp2(st - m) for st in stats]
    den = functools.reduce(lambda a, b: a + b, [e * pltpu.roll(st, HEAD_DIM, 1) for e, st in zip(es, stats)])
    inv = jnp.where(is_head, 1.0 / den, 0.0)
    yb = None
    for e, o_ref, dil, scr in zip(es, ob_refs, dils, scr_o_refs):
        alpha = jnp.dot((e * inv).astype(BF16), expand_ref[...], preferred_element_type=F32)
        term = alpha * _load_by_class(o_ref, dil, scr, B_WIDTH)
        yb = term if yb is None else yb + term
    yb = (yb * sgb_ref[...].astype(F32)).astype(BF16)
    br_a = jnp.dot(ya_ref[...], wa_ref[...], preferred_element_type=F32)
    br_b = jnp.dot(yb, wb_ref[...], preferred_element_type=F32)
    merged = (gate_ref[:, :D_MODEL].astype(F32) * br_a
              + gate_ref[:, D_MODEL:].astype(F32) * br_b).astype(BF16)
    y_ref[...] = x_ref[...] + jnp.dot(merged, wo_ref[...], preferred_element_type=F32)


def _merge_out(x2, qa, ka, va, bias_rows, sink, sga, obs, stats, sgb, gates, expand, wa, wb, wo, *, tm, seq_len):
    m = x2.shape[0]
    tiles_per_seq = seq_len // tm

    def rows(width):
        return pl.BlockSpec((tm, width), lambda i: (i, 0))

    def whole(a):
        return pl.BlockSpec(a.shape, lambda i: (0,) * a.ndim)

    def by_class(width):
        return [pl.BlockSpec((tm // dil, dil * width), lambda i: (i, 0)) for _, dil in B_GROUPS]

    seq_kv = pl.BlockSpec((1, seq_len, PAIR), lambda i: (i // tiles_per_seq, 0, 0))
    n_g = len(B_GROUPS)
    stack = A_Q_HEADS // 2
    scratch = [pltpu.VMEM((3, 1, 2 * stack * Q_BLK, Q_BLK + 2 * A_WINDOW), F32),
               pltpu.VMEM((tm, A_WIDTH), BF16),
               [pltpu.VMEM((B_WIDTH // LANES, tm, LANES), F32)] * n_g,
               [pltpu.VMEM((1, tm, LANES), F32)] * n_g]
    return pl.pallas_call(
        functools.partial(_merge_out_kernel, seq_len=seq_len),
        grid=(m // tm,),
        in_specs=[rows(D_MODEL), rows(A_WIDTH), seq_kv, seq_kv, whole(bias_rows),
                  pl.BlockSpec(memory_space=pltpu.SMEM), rows(A_WIDTH), by_class(B_WIDTH),
                  by_class(LANES), rows(B_WIDTH), rows(2 * D_MODEL),
                  whole(expand), whole(wa), whole(wb), whole(wo)],
        out_specs=rows(D_MODEL),
        out_shape=jax.ShapeDtypeStruct((m, D_MODEL), F32),
        scratch_shapes=scratch,
        compiler_params=pltpu.CompilerParams(vmem_limit_bytes=VMEM_LIMIT, dimension_semantics=("arbitrary",)),
        name="merge_out",
    )(x2, qa, ka, va, bias_rows, sink.astype(F32), sga, obs, stats, sgb, gates, expand, wa, wb, wo)


def _t5_bucket(rel):
    half = N_BUCKETS // 2
    max_exact = half // 2
    ret = (rel > 0).astype(jnp.int32) * half
    n = jnp.abs(rel)
    nf = jnp.maximum(n, max_exact).astype(jnp.float32)
    large = max_exact + jnp.floor(jnp.log(nf / max_exact) / math.log(MAX_DISTANCE / max_exact)
                                  * (half - max_exact)).astype(jnp.int32)
    large = jnp.minimum(large, half - 1)
    return ret + jnp.where(n < max_exact, n, large)


def _bias_rows(table, half_window, stride):
    n_rel = 2 * Q_BLK - 1 + 4 * half_window
    width = -(-n_rel // LANES) * LANES
    rel = jnp.arange(width) - (Q_BLK - 1 + 2 * half_window)
    bucket = _t5_bucket(rel * stride)[None, :]
    tab = table.astype(F32) * LOG2E
    rows = jnp.zeros((table.shape[1], width), F32)
    for b in range(N_BUCKETS):
        rows = jnp.where(bucket == b, tab[b][:, None], rows)
    return jnp.where((jnp.abs(rel) <= half_window)[None, :], rows, NEG_INF)


_A_HEAD_ORDER = tuple(h for p in range(A_Q_HEADS // 2) for h in (p, p + A_Q_HEADS // 2))


def kernel(x, norm_gain, w_in, q_norm_a, k_norm_a, q_norm_b, k_norm_b, sink_a, rel_bias,
           w_branch_a, w_branch_b, b_merge, w_out):
    bsz, seq, d = x.shape
    scale = HEAD_DIM ** -0.5

    expand = jnp.asarray(np.kron(np.eye(LANES, B_HEADS), np.ones((1, HEAD_DIM))), BF16)
    for layer in range(norm_gain.shape[0]):
        w = w_in[layer].astype(BF16)
        wa = jnp.concatenate([w_branch_a[layer][h * HEAD_DIM:(h + 1) * HEAD_DIM] for h in _A_HEAD_ORDER],
                             axis=0).astype(BF16)
        wb = w_branch_b[layer].astype(BF16)
        wo = w_out[layer].astype(BF16)
        gqa = jnp.tile(q_norm_a[layer] * (scale * LOG2E), A_WIDTH // HEAD_DIM)[None]
        gka = jnp.tile(k_norm_a[layer], A_KV_HEADS)[None]
        gqb = jnp.tile(q_norm_b[layer] * (scale * LOG2E), B_HEADS)[None]
        gkb = jnp.tile(k_norm_b[layer], B_HEADS)[None]
        bm = b_merge[layer].reshape(1, -1)
        sink = sink_a[layer][np.asarray(_A_HEAD_ORDER)] * LOG2E

        x2 = x.reshape(bsz * seq, d)
        qa, ka, va, qbs, kbs, vbs, sga, sgb, gates = _in_proj(
            x2, norm_gain[layer][None], w, gqa, gka, gqb, gkb, bm, tm=512)

        obs, stats = [], []
        for g, (window, dil) in enumerate(B_GROUPS):
            c0 = A_Q_HEADS + g * B_HEADS
            hw = window // (2 * dil)
            sub = seq // dil
            bias_g = _bias_rows(rel_bias[:, c0:c0 + B_HEADS], hw, dil)
            view = (bsz, sub, dil * B_WIDTH)
            tq = min(sub, B_ROWS_PER_STEP)
            n_cls = min(dil, B_ROWS_PER_STEP // tq)
            unroll = min(tq // Q_BLK, B_UNITS_PER_BODY // (n_cls * B_HEADS))
            o_g, st_g = _band_attn(qbs[g].reshape(view), kbs[g].reshape(view), vbs[g].reshape(view), bias_g,
                                   seq_len=sub, tq=tq, half_window=hw, n_cls=n_cls, unroll=unroll)
            obs.append(o_g.reshape(bsz * sub, dil * B_WIDTH))
            stats.append(st_g.reshape(bsz * sub, dil * LANES))

        bias_a = _bias_rows(rel_bias[:, :A_Q_HEADS][:, np.asarray(_A_HEAD_ORDER)], A_WINDOW, 1)
        y2 = _merge_out(x2, qa, ka.reshape(bsz, seq, PAIR), va.reshape(bsz, seq, PAIR), bias_a, sink,
                        sga, obs, stats, sgb, gates, expand, wa, wb, wo, tm=512, seq_len=seq)
        x = y2.reshape(bsz, seq, d)
    return x
```

```python
import functools
import math

import jax
import jax.numpy as jnp
import numpy as np
from jax import lax
from jax.experimental import pallas as pl
from jax.experimental.pallas import tpu as pltpu

D_MODEL = 1024
HEAD_DIM = 64
A_Q_HEADS = 8
A_KV_HEADS = 2
A_WINDOW = 128
B_GROUPS = ((128, 1), (512, 4), (2048, 16))
B_HEADS = 8
A_WIDTH = A_Q_HEADS * HEAD_DIM
B_WIDTH = B_HEADS * HEAD_DIM
N_BUCKETS = 32
MAX_DISTANCE = 1024
EPS = 1e-6
NEG_INF = -1e30
LOG2E = 1.0 / math.log(2.0)

LANES = 128
PAIR = 2 * HEAD_DIM
Q_BLK = 128
SEG = 256
ROW_TILE = 512
B_ROWS_PER_STEP = 2048
B_UNITS_PER_BODY = 128
VMEM_LIMIT = 56 * 1024 * 1024

BF16 = jnp.bfloat16
F32 = jnp.float32


def _head_norm(y, gain):
    lo = lax.broadcasted_iota(jnp.int32, (1, PAIR), 1) < HEAD_DIM
    parts = []
    for c in range(0, y.shape[1], PAIR):
        yc = y[:, c:c + PAIR]
        sq = yc * yc
        s_lo = jnp.sum(jnp.where(lo, sq, 0.0), axis=-1, keepdims=True)
        s_hi = jnp.sum(jnp.where(lo, 0.0, sq), axis=-1, keepdims=True)
        ms = jnp.where(lo, s_lo, s_hi) * (1.0 / HEAD_DIM)
        parts.append(yc * lax.rsqrt(ms + EPS))
    yn = parts[0] if len(parts) == 1 else jnp.concatenate(parts, axis=1)
    return yn * gain


def _silu(y):
    return y * (1.0 / (1.0 + jnp.exp(-y)))


def _pair_order(y):
    lo = lax.broadcasted_iota(jnp.int32, (1, PAIR), 1) < HEAD_DIM
    n_pairs = A_Q_HEADS // 2
    cols = [y[:, g * PAIR:(g + 1) * PAIR] for g in range(n_pairs)]
    swapped = [pltpu.roll(col, HEAD_DIM, 1) for col in cols]
    out = []
    for p in range(n_pairs):
        first, second = p, p + n_pairs
        lo_src = cols[first // 2] if first % 2 == 0 else swapped[first // 2]
        hi_src = cols[second // 2] if second % 2 == 1 else swapped[second // 2]
        out.append(jnp.where(lo, lo_src, hi_src))
    return jnp.concatenate(out, axis=1)


def _store_by_class(y, dil, scr_ref, out_ref, col0, width):
    rows, w = y.shape
    if dil == 1:
        out_ref[:, col0:col0 + w] = y.astype(out_ref.dtype)
        return
    slabs = w // LANES
    for s in range(slabs):
        scr_ref[s] = y[:, s * LANES:(s + 1) * LANES]
    for c in range(dil):
        for s in range(slabs):
            col = c * width + col0 + s * LANES
            out_ref[:, col:col + LANES] = scr_ref[s, pl.ds(c, rows // dil, stride=dil), :].astype(out_ref.dtype)


def _load_by_class(ref, dil, scr_ref, width):
    if dil == 1:
        return ref[...].astype(F32)
    sub_rows = ref.shape[0]
    slabs = width // LANES
    for c in range(dil):
        for s in range(slabs):
            col = c * width + s * LANES
            scr_ref[s, pl.ds(c, sub_rows, stride=dil), :] = ref[:, col:col + LANES].astype(F32)
    parts = [scr_ref[s] for s in range(slabs)]
    return parts[0] if slabs == 1 else jnp.concatenate(parts, axis=1)


def _in_proj_kernel(x_ref, ng_ref, w_ref, gqa_ref, gka_ref, gqb_ref, gkb_ref,
                    qa_ref, ka_ref, va_ref, qb_refs, kb_refs, vb_refs, sga_ref, sgb_ref,
                    scr_refs):
    x = x_ref[...]
    ms = jnp.mean(x * x, axis=-1, keepdims=True)
    h = ((x * lax.rsqrt(ms + EPS)) * ng_ref[...]).astype(BF16)

    def proj(c0):
        return jnp.dot(h, w_ref[:, c0:c0 + SEG], preferred_element_type=F32)

    n_g = len(B_GROUPS)
    c_kva = A_WIDTH
    c_b = c_kva + 2 * PAIR
    c_ga = c_b + 3 * n_g * B_WIDTH
    c_gb = c_ga + A_WIDTH
    offs = range(0, B_WIDTH, SEG)

    def sgb_job(off):
        def job():
            sgb_ref[:, off:off + SEG] = _silu(proj(c_gb + off)).astype(BF16)
        return job

    def sga_job():
        sga = [_silu(proj(c_ga + off)) for off in range(0, A_WIDTH, SEG)]
        sga_ref[...] = _pair_order(jnp.concatenate(sga, axis=1)).astype(BF16)

    def qa_job():
        qa = [_head_norm(proj(off), gqa_ref[:, off:off + SEG]) for off in range(0, A_WIDTH, SEG)]
        qa_ref[...] = _pair_order(jnp.concatenate(qa, axis=1)).astype(BF16)

    def kva_job():
        y = proj(c_kva)
        ka_ref[...] = _head_norm(y[:, :PAIR], gka_ref[...]).astype(BF16)
        va_ref[...] = y[:, PAIR:].astype(BF16)

    n_scr = [0]

    def group_job(part, refs, gain_ref, g, off):
        def job():
            dil = B_GROUPS[g][1]
            y = proj(c_b + (part * n_g + g) * B_WIDTH + off)
            if gain_ref is not None:
                y = _head_norm(y, gain_ref[:, off:off + SEG])
            _store_by_class(y, dil, scr_refs[n_scr[0] % len(scr_refs)], refs[g], off, B_WIDTH)
            n_scr[0] += dil > 1
        return job

    exp_jobs = [sga_job] + [sgb_job(off) for off in offs]
    norm_jobs = [group_job(p, refs, gain, g, off) for p, (refs, gain) in enumerate(((qb_refs, gqb_ref), (kb_refs, gkb_ref)))
                 for g in range(n_g) for off in offs] + [qa_job]
    plain_jobs = [group_job(2, vb_refs, None, g, off) for g in range(n_g) for off in offs] + [kva_job]
    order = []
    for i in range(max(len(exp_jobs), len(norm_jobs))):
        order += exp_jobs[i:i + 1] + norm_jobs[i:i + 1]
    for job in order + plain_jobs:
        job()


def _in_proj(x2, ng, w, gqa, gka, gqb, gkb, tm):
    m = x2.shape[0]

    def rows(width):
        return pl.BlockSpec((tm, width), lambda i: (i, 0))

    def whole(a):
        return pl.BlockSpec(a.shape, lambda i: (0,) * a.ndim)

    def out(width):
        return jax.ShapeDtypeStruct((m, width), BF16)

    b_shape = [jax.ShapeDtypeStruct((m // dil, dil * B_WIDTH), BF16) for _, dil in B_GROUPS]
    b_spec = [pl.BlockSpec((tm // dil, dil * B_WIDTH), lambda i: (i, 0)) for _, dil in B_GROUPS]
    out_shape = (out(A_WIDTH), out(PAIR), out(PAIR), b_shape, b_shape, b_shape, out(A_WIDTH), out(B_WIDTH))
    out_specs = (rows(A_WIDTH), rows(PAIR), rows(PAIR), b_spec, b_spec, b_spec, rows(A_WIDTH), rows(B_WIDTH))
    scratch = [[pltpu.VMEM((SEG // LANES, tm, LANES), F32)] * 2]
    return pl.pallas_call(
        _in_proj_kernel,
        grid=(m // tm,),
        in_specs=[rows(D_MODEL), whole(ng), pl.BlockSpec(memory_space=pltpu.VMEM),
                  whole(gqa), whole(gka), whole(gqb), whole(gkb)],
        out_specs=out_specs,
        out_shape=out_shape,
        scratch_shapes=scratch,
        compiler_params=pltpu.CompilerParams(vmem_limit_bytes=VMEM_LIMIT),
        name="in_proj",
    )(x2, ng, w, gqa, gka, gqb, gkb)


def _init_bias_tiles(bias_row_ref, bias_ref, *, half_window, n_heads, stack):
    tk = Q_BLK + 2 * half_window
    width = bias_row_ref.shape[1]
    for li, lead in enumerate((0, half_window, 2 * half_window)):
        shift = width - (Q_BLK - 1 + 2 * half_window - lead)
        for head in range(n_heads):
            base = jnp.broadcast_to(bias_row_ref[head:head + 1, :], (Q_BLK, width))
            tile = pltpu.roll(base, shift, 1, stride=1, stride_axis=0)
            grp, idx = divmod(head, 2 * stack)
            bias_ref[li, grp, idx * Q_BLK:(idx + 1) * Q_BLK, :] = tile[:, :tk]


def _attn_block(blk, load_q, load_kv, bias_ref, sink_ref, store_out, store_stats, *,
                seq_len, half_window, n_cls, q_pairs, kv_pairs, stack):
    tk = Q_BLK + 2 * half_window
    n_blk = seq_len // Q_BLK
    lo = lax.broadcasted_iota(jnp.int32, (1, PAIR), 1) < HEAD_DIM
    lane_l = lax.broadcasted_iota(jnp.int32, (1, LANES), 1)
    ws = pl.multiple_of(jnp.clip(blk * Q_BLK - half_window, 0, seq_len - tk), HEAD_DIM)
    sel = (blk > 0).astype(jnp.int32) + (blk == n_blk - 1).astype(jnp.int32)
    ones = jnp.ones((tk, PAIR), BF16)
    for c in range(n_cls):
        st_tile = jnp.where(lane_l < HEAD_DIM, jnp.zeros((Q_BLK, LANES), F32), 1.0)
        for grp in range(q_pairs // stack):
            k, v = load_kv(ws, (c * kv_pairs + grp * stack * kv_pairs // q_pairs) * PAIR)
            qcols = [(c * q_pairs + grp * stack + i) * PAIR for i in range(stack)]
            qs = []
            for qcol in qcols:
                q = load_q(qcol)
                zero = jnp.zeros_like(q)
                qs += [jnp.where(lo, q, zero), jnp.where(lo, zero, q)]
            s = lax.dot_general(jnp.concatenate(qs, axis=0), k, (((1,), (1,)), ((), ())),
                                preferred_element_type=F32)
            s = s + bias_ref[sel, grp]
            heads = range(2 * stack * grp, 2 * stack * (grp + 1))
            ms, es = [], []
            for i, head in enumerate(heads):
                s_h = s[i * Q_BLK:(i + 1) * Q_BLK]
                m_h = jnp.max(s_h, axis=-1, keepdims=True)
                if sink_ref is not None:
                    m_h = jnp.maximum(m_h, sink_ref[head])
                ms.append(m_h)
                es.append(jnp.exp2(s_h - m_h).astype(BF16))
            pv = jnp.dot(jnp.concatenate(es, axis=0), jnp.concatenate([v, ones], axis=1),
                         preferred_element_type=F32)
            outs = []
            for i, head in enumerate(heads):
                l_h = pv[i * Q_BLK:(i + 1) * Q_BLK, PAIR:]
                if sink_ref is not None:
                    l_h = l_h + jnp.exp2(sink_ref[head] - ms[i])
                if store_stats is not None:
                    outs.append(pv[i * Q_BLK:(i + 1) * Q_BLK, :PAIR])
                    st_tile = jnp.where(lane_l == head, ms[i], st_tile)
                    st_tile = jnp.where(lane_l == HEAD_DIM + head, l_h, st_tile)
                else:
                    outs.append(pv[i * Q_BLK:(i + 1) * Q_BLK, :PAIR] * (1.0 / l_h))
            for i, qcol in enumerate(qcols):
                store_out(qcol, jnp.where(lo, outs[2 * i], outs[2 * i + 1]))
        if store_stats is not None:
            store_stats(c, st_tile)


def _band_attn_kernel(q_ref, k_ref, v_ref, bias_row_ref, o_ref, st_ref, bias_ref, *,
                      seq_len, tq, half_window, n_cls, q_pairs, unroll):
    t = pl.program_id(2)

    @pl.when((pl.program_id(0) == 0) & (pl.program_id(1) == 0) & (t == 0))
    def _():
        _init_bias_tiles(bias_row_ref, bias_ref, half_window=half_window, n_heads=2 * q_pairs, stack=1)

    tk = Q_BLK + 2 * half_window

    def block(j, carry):
        row = pl.multiple_of(j * Q_BLK, Q_BLK)

        def store_out(col, x):
            o_ref[0, pl.ds(row, Q_BLK), col:col + PAIR] = x.astype(BF16)

        def store_stats(c, tile):
            st_ref[0, pl.ds(row, Q_BLK), c * LANES:(c + 1) * LANES] = tile

        _attn_block(t * (tq // Q_BLK) + j,
                    lambda col: q_ref[0, pl.ds(row, Q_BLK), col:col + PAIR],
                    lambda ws, col: (k_ref[0, pl.ds(ws, tk), col:col + PAIR],
                                     v_ref[0, pl.ds(ws, tk), col:col + PAIR]),
                    bias_ref, None, store_out, store_stats, seq_len=seq_len, half_window=half_window,
                    n_cls=n_cls, q_pairs=q_pairs, kv_pairs=q_pairs, stack=1)
        return carry

    lax.fori_loop(0, tq // Q_BLK, block, 0, unroll=unroll)


def _band_attn(q, k, v, bias_rows, *, seq_len, tq, half_window, n_cls, unroll):
    n_rows, _, cols = q.shape
    tk = Q_BLK + 2 * half_window
    q_pairs = bias_rows.shape[0] // 2
    n_col = cols // (q_pairs * PAIR)
    assert k.shape == q.shape and v.shape == q.shape and bias_rows.shape[1] % LANES == 0
    assert seq_len % tq == 0 and tq % Q_BLK == 0 and n_col % n_cls == 0

    blk = n_cls * q_pairs * PAIR
    rows_spec = pl.BlockSpec((1, tq, blk), lambda r, c, t: (r, t, c))
    seq_spec = pl.BlockSpec((1, seq_len, blk), lambda r, c, t: (r, 0, c))
    kern = functools.partial(_band_attn_kernel, seq_len=seq_len, tq=tq, half_window=half_window,
                             n_cls=n_cls, q_pairs=q_pairs, unroll=unroll)
    return pl.pallas_call(
        kern,
        grid=(n_rows, n_col // n_cls, seq_len // tq),
        in_specs=[rows_spec, seq_spec, seq_spec, pl.BlockSpec(bias_rows.shape, lambda r, c, t: (0, 0))],
        out_specs=[rows_spec, pl.BlockSpec((1, tq, n_cls * LANES), lambda r, c, t: (r, t, c))],
        out_shape=[jax.ShapeDtypeStruct(q.shape, BF16),
                   jax.ShapeDtypeStruct((n_rows, seq_len, n_col * LANES), F32)],
        scratch_shapes=[pltpu.VMEM((3, q_pairs, 2 * Q_BLK, tk), F32)],
        compiler_params=pltpu.CompilerParams(vmem_limit_bytes=VMEM_LIMIT,
                                             dimension_semantics=("arbitrary",) * 3),
        name=f"band_attn_len{seq_len}",
    )(q, k, v, bias_rows)


def _merge_out_kernel(x_ref, ng_ref, qa_ref, ka_ref, va_ref, bias_row_ref, sink_ref, sga_ref, ob_refs, st_refs,
                      sgb_ref, wg_ref, bm_ref, expand_ref, wa_ref, wb_ref, wo_ref, y_ref,
                      bias_ref, ya_ref, gate_ref, scr_o_refs, scr_l_refs, *, seq_len):
    i = pl.program_id(0)
    tm = x_ref.shape[0]
    stack = A_Q_HEADS // 2

    @pl.when(i == 0)
    def _():
        _init_bias_tiles(bias_row_ref, bias_ref, half_window=A_WINDOW, n_heads=A_Q_HEADS, stack=stack)

    x = x_ref[...]
    ms = jnp.mean(x * x, axis=-1, keepdims=True)
    h = ((x * lax.rsqrt(ms + EPS)) * ng_ref[...]).astype(BF16)
    for c0 in range(0, 2 * D_MODEL, SEG):
        logit = jnp.dot(h, wg_ref[:, c0:c0 + SEG], preferred_element_type=F32) + bm_ref[:, c0:c0 + SEG]
        gate_ref[:, c0:c0 + SEG] = (1.0 / (1.0 + jnp.exp(-logit))).astype(BF16)

    first_blk = lax.rem(i, seq_len // tm) * (tm // Q_BLK)
    tk = Q_BLK + 2 * A_WINDOW

    def mixer_a_block(j):
        rows = slice(j * Q_BLK, (j + 1) * Q_BLK)

        def store_out(col, x):
            ya_ref[rows, col:col + PAIR] = (x * sga_ref[rows, col:col + PAIR].astype(F32)).astype(BF16)

        _attn_block(first_blk + j,
                    lambda col: qa_ref[rows, col:col + PAIR],
                    lambda ws, col: (ka_ref[0, pl.ds(ws, tk), col:col + PAIR],
                                     va_ref[0, pl.ds(ws, tk), col:col + PAIR]),
                    bias_ref, sink_ref, store_out, None, seq_len=seq_len, half_window=A_WINDOW, n_cls=1,
                    q_pairs=stack, kv_pairs=A_KV_HEADS // 2, stack=stack)

    def mixer_b_branch():
        dils = [dil for _, dil in B_GROUPS]
        stats = [_load_by_class(r, dil, s, LANES) for r, dil, s in zip(st_refs, dils, scr_l_refs)]
        is_head = lax.broadcasted_iota(jnp.int32, (1, LANES), 1) < HEAD_DIM
        m = functools.reduce(jnp.maximum, stats)
        es = [jnp.exp2(st - m) for st in stats]
        den = functools.reduce(lambda a, b: a + b, [e * pltpu.roll(st, HEAD_DIM, 1) for e, st in zip(es, stats)])
        inv = jnp.where(is_head, 1.0 / den, 0.0)
        yb = None
        for e, o_ref, dil, scr in zip(es, ob_refs, dils, scr_o_refs):
            alpha = jnp.dot((e * inv).astype(BF16), expand_ref[...], preferred_element_type=F32)
            term = alpha * _load_by_class(o_ref, dil, scr, B_WIDTH)
            yb = term if yb is None else yb + term
        return (yb * sgb_ref[...].astype(F32)).astype(BF16)

    for j in range(tm // Q_BLK):
        mixer_a_block(j)
    yb = mixer_b_branch()
    br_a = jnp.dot(ya_ref[...], wa_ref[...], preferred_element_type=F32)
    br_b = jnp.dot(yb, wb_ref[...], preferred_element_type=F32)
    merged = (gate_ref[:, :D_MODEL].astype(F32) * br_a
              + gate_ref[:, D_MODEL:].astype(F32) * br_b).astype(BF16)
    y_ref[...] = x_ref[...] + jnp.dot(merged, wo_ref[...], preferred_element_type=F32)


def _merge_out(x2, ng, qa, ka, va, bias_rows, sink, sga, obs, stats, sgb, wg, bm, expand, wa, wb, wo, *, tm, seq_len):
    m = x2.shape[0]
    tiles_per_seq = seq_len // tm

    def rows(width):
        return pl.BlockSpec((tm, width), lambda i: (i, 0))

    def whole(a):
        return pl.BlockSpec(a.shape, lambda i: (0,) * a.ndim)

    def by_class(width):
        return [pl.BlockSpec((tm // dil, dil * width), lambda i: (i, 0)) for _, dil in B_GROUPS]

    seq_kv = pl.BlockSpec((1, seq_len, PAIR), lambda i: (i // tiles_per_seq, 0, 0))
    n_g = len(B_GROUPS)
    stack = A_Q_HEADS // 2
    scratch = [pltpu.VMEM((3, 1, 2 * stack * Q_BLK, Q_BLK + 2 * A_WINDOW), F32),
               pltpu.VMEM((tm, A_WIDTH), BF16),
               pltpu.VMEM((tm, 2 * D_MODEL), BF16),
               [pltpu.VMEM((B_WIDTH // LANES, tm, LANES), F32)] * n_g,
               [pltpu.VMEM((1, tm, LANES), F32)] * n_g]
    return pl.pallas_call(
        functools.partial(_merge_out_kernel, seq_len=seq_len),
        grid=(m // tm,),
        in_specs=[rows(D_MODEL), whole(ng), rows(A_WIDTH), seq_kv, seq_kv, whole(bias_rows),
                  pl.BlockSpec(memory_space=pltpu.SMEM), rows(A_WIDTH), by_class(B_WIDTH),
                  by_class(LANES), rows(B_WIDTH), whole(wg), whole(bm),
                  whole(expand), whole(wa), whole(wb), whole(wo)],
        out_specs=rows(D_MODEL),
        out_shape=jax.ShapeDtypeStruct((m, D_MODEL), F32),
        scratch_shapes=scratch,
        compiler_params=pltpu.CompilerParams(vmem_limit_bytes=VMEM_LIMIT, dimension_semantics=("arbitrary",)),
        name="merge_out",
    )(x2, ng, qa, ka, va, bias_rows, sink.astype(F32), sga, obs, stats, sgb, wg, bm, expand, wa, wb, wo)


def _t5_bucket(rel):
    half = N_BUCKETS // 2
    max_exact = half // 2
    ret = (rel > 0).astype(jnp.int32) * half
    n = jnp.abs(rel)
    nf = jnp.maximum(n, max_exact).astype(jnp.float32)
    large = max_exact + jnp.floor(jnp.log(nf / max_exact) / math.log(MAX_DISTANCE / max_exact)
                                  * (half - max_exact)).astype(jnp.int32)
    large = jnp.minimum(large, half - 1)
    return ret + jnp.where(n < max_exact, n, large)


def _bias_rows(table, half_window, stride):
    n_rel = 2 * Q_BLK - 1 + 4 * half_window
    width = -(-n_rel // LANES) * LANES
    rel = jnp.arange(width) - (Q_BLK - 1 + 2 * half_window)
    bucket = _t5_bucket(rel * stride)[None, :]
    tab = table.astype(F32) * LOG2E
    rows = jnp.zeros((table.shape[1], width), F32)
    for b in range(N_BUCKETS):
        rows = jnp.where(bucket == b, tab[b][:, None], rows)
    return jnp.where((jnp.abs(rel) <= half_window)[None, :], rows, NEG_INF)


_A_HEAD_ORDER = tuple(h for p in range(A_Q_HEADS // 2) for h in (p, p + A_Q_HEADS // 2))


def kernel(x, norm_gain, w_in, q_norm_a, k_norm_a, q_norm_b, k_norm_b, sink_a, rel_bias,
           w_branch_a, w_branch_b, b_merge, w_out):
    bsz, seq, d = x.shape
    scale = HEAD_DIM ** -0.5

    expand = jnp.asarray(np.kron(np.eye(LANES, B_HEADS), np.ones((1, HEAD_DIM))), BF16)
    for layer in range(norm_gain.shape[0]):
        c_mg = w_in.shape[2] - 2 * d
        w = w_in[layer][:, :c_mg].astype(BF16)
        wg = w_in[layer][:, c_mg:].astype(BF16)
        wa = jnp.concatenate([w_branch_a[layer][h * HEAD_DIM:(h + 1) * HEAD_DIM] for h in _A_HEAD_ORDER],
                             axis=0).astype(BF16)
        wb = w_branch_b[layer].astype(BF16)
        wo = w_out[layer].astype(BF16)
        gqa = jnp.tile(q_norm_a[layer] * (scale * LOG2E), A_WIDTH // HEAD_DIM)[None]
        gka = jnp.tile(k_norm_a[layer], A_KV_HEADS)[None]
        gqb = jnp.tile(q_norm_b[layer] * (scale * LOG2E), B_HEADS)[None]
        gkb = jnp.tile(k_norm_b[layer], B_HEADS)[None]
        bm = b_merge[layer].reshape(1, -1)
        sink = sink_a[layer][np.asarray(_A_HEAD_ORDER)] * LOG2E

        x2 = x.reshape(bsz * seq, d)
        qa, ka, va, qbs, kbs, vbs, sga, sgb = _in_proj(
            x2, norm_gain[layer][None], w, gqa, gka, gqb, gkb, tm=ROW_TILE)

        obs, stats = [], []
        for g, (window, dil) in enumerate(B_GROUPS):
            c0 = A_Q_HEADS + g * B_HEADS
            hw = window // (2 * dil)
            sub = seq // dil
            bias_g = _bias_rows(rel_bias[:, c0:c0 + B_HEADS], hw, dil)
            view = (bsz, sub, dil * B_WIDTH)
            tq = min(sub, B_ROWS_PER_STEP)
            n_cls = min(dil, B_ROWS_PER_STEP // tq)
            unroll = min(tq // Q_BLK, B_UNITS_PER_BODY // (n_cls * B_HEADS))
            o_g, st_g = _band_attn(qbs[g].reshape(view), kbs[g].reshape(view), vbs[g].reshape(view), bias_g,
                                   seq_len=sub, tq=tq, half_window=hw, n_cls=n_cls, unroll=unroll)
            obs.append(o_g.reshape(bsz * sub, dil * B_WIDTH))
            stats.append(st_g.reshape(bsz * sub, dil * LANES))

        bias_a = _bias_rows(rel_bias[:, :A_Q_HEADS][:, np.asarray(_A_HEAD_ORDER)], A_WINDOW, 1)
        y2 = _merge_out(x2, norm_gain[layer][None], qa, ka.reshape(bsz, seq, PAIR), va.reshape(bsz, seq, PAIR),
                        bias_a, sink, sga, obs, stats, sgb, wg, bm, expand, wa, wb, wo, tm=ROW_TILE, seq_len=seq)
        x = y2.reshape(bsz, seq, d)
    return x
```

```python
import functools
import math

import jax
import jax.numpy as jnp
import numpy as np
from jax import lax
from jax.experimental import pallas as pl
from jax.experimental.pallas import tpu as pltpu

D_MODEL = 1024
HEAD_DIM = 64
A_Q_HEADS = 8
A_KV_HEADS = 2
A_WINDOW = 128
B_GROUPS = ((128, 1), (512, 4), (2048, 16))
B_HEADS = 8
A_WIDTH = A_Q_HEADS * HEAD_DIM
B_WIDTH = B_HEADS * HEAD_DIM
N_BUCKETS = 32
MAX_DISTANCE = 1024
EPS = 1e-6
NEG_INF = -1e30
LOG2E = 1.0 / math.log(2.0)

LANES = 128
PAIR = 2 * HEAD_DIM
Q_BLK = 128
SEG = 256
ROW_TILE = 512
B_ROWS_PER_STEP = 2048
B_UNITS_PER_BODY = 128
VMEM_LIMIT = 56 * 1024 * 1024

BF16 = jnp.bfloat16
F32 = jnp.float32


def _head_norm(y, gain):
    lo = lax.broadcasted_iota(jnp.int32, (1, PAIR), 1) < HEAD_DIM
    parts = []
    for c in range(0, y.shape[1], PAIR):
        yc = y[:, c:c + PAIR]
        sq = yc * yc
        s_lo = jnp.sum(jnp.where(lo, sq, 0.0), axis=-1, keepdims=True)
        s_hi = jnp.sum(jnp.where(lo, 0.0, sq), axis=-1, keepdims=True)
        ms = jnp.where(lo, s_lo, s_hi) * (1.0 / HEAD_DIM)
        parts.append(yc * lax.rsqrt(ms + EPS))
    yn = parts[0] if len(parts) == 1 else jnp.concatenate(parts, axis=1)
    return yn * gain


def _silu(y):
    return y * (1.0 / (1.0 + jnp.exp(-y)))


def _pair_order(y):
    lo = lax.broadcasted_iota(jnp.int32, (1, PAIR), 1) < HEAD_DIM
    n_pairs = A_Q_HEADS // 2
    cols = [y[:, g * PAIR:(g + 1) * PAIR] for g in range(n_pairs)]
    swapped = [pltpu.roll(col, HEAD_DIM, 1) for col in cols]
    out = []
    for p in range(n_pairs):
        first, second = p, p + n_pairs
        lo_src = cols[first // 2] if first % 2 == 0 else swapped[first // 2]
        hi_src = cols[second // 2] if second % 2 == 1 else swapped[second // 2]
        out.append(jnp.where(lo, lo_src, hi_src))
    return jnp.concatenate(out, axis=1)


def _store_by_class(y, dil, scr_ref, out_ref, col0, width):
    rows, w = y.shape
    if dil == 1:
        out_ref[:, col0:col0 + w] = y.astype(out_ref.dtype)
        return
    slabs = w // LANES
    for s in range(slabs):
        scr_ref[s] = y[:, s * LANES:(s + 1) * LANES]
    for c in range(dil):
        for s in range(slabs):
            col = c * width + col0 + s * LANES
            out_ref[:, col:col + LANES] = scr_ref[s, pl.ds(c, rows // dil, stride=dil), :].astype(out_ref.dtype)


def _load_by_class(ref, dil, scr_ref, width):
    if dil == 1:
        return ref[...].astype(F32)
    sub_rows = ref.shape[0]
    slabs = width // LANES
    for c in range(dil):
        for s in range(slabs):
            col = c * width + s * LANES
            scr_ref[s, pl.ds(c, sub_rows, stride=dil), :] = ref[:, col:col + LANES].astype(F32)
    parts = [scr_ref[s] for s in range(slabs)]
    return parts[0] if slabs == 1 else jnp.concatenate(parts, axis=1)


def _in_proj_kernel(x_ref, ng_ref, w_ref, gqa_ref, gka_ref, gqb_ref, gkb_ref, bm_ref,
                    qa_ref, ka_ref, va_ref, qb_refs, kb_refs, vb_refs, sga_ref, sgb_ref, gate_ref,
                    scr_refs):
    x = x_ref[...]
    ms = jnp.mean(x * x, axis=-1, keepdims=True)
    h = ((x * lax.rsqrt(ms + EPS)) * ng_ref[...]).astype(BF16)

    def proj(c0):
        return jnp.dot(h, w_ref[:, c0:c0 + SEG], preferred_element_type=F32)

    n_g = len(B_GROUPS)
    c_kva = A_WIDTH
    c_b = c_kva + 2 * PAIR
    c_ga = c_b + 3 * n_g * B_WIDTH
    c_gb = c_ga + A_WIDTH
    c_mg = c_gb + B_WIDTH
    offs = range(0, B_WIDTH, SEG)

    def gate_job(off):
        def job():
            y = proj(c_mg + off) + bm_ref[:, off:off + SEG]
            gate_ref[:, off:off + SEG] = (1.0 / (1.0 + jnp.exp(-y))).astype(BF16)
        return job

    def sgb_job(off):
        def job():
            sgb_ref[:, off:off + SEG] = _silu(proj(c_gb + off)).astype(BF16)
        return job

    def sga_job():
        sga = [_silu(proj(c_ga + off)) for off in range(0, A_WIDTH, SEG)]
        sga_ref[...] = _pair_order(jnp.concatenate(sga, axis=1)).astype(BF16)

    def qa_job():
        qa = [_head_norm(proj(off), gqa_ref[:, off:off + SEG]) for off in range(0, A_WIDTH, SEG)]
        qa_ref[...] = _pair_order(jnp.concatenate(qa, axis=1)).astype(BF16)

    def kva_job():
        y = proj(c_kva)
        ka_ref[...] = _head_norm(y[:, :PAIR], gka_ref[...]).astype(BF16)
        va_ref[...] = y[:, PAIR:].astype(BF16)

    n_scr = [0]

    def group_job(part, refs, gain_ref, g, off):
        def job():
            dil = B_GROUPS[g][1]
            y = proj(c_b + (part * n_g + g) * B_WIDTH + off)
            if gain_ref is not None:
                y = _head_norm(y, gain_ref[:, off:off + SEG])
            _store_by_class(y, dil, scr_refs[n_scr[0] % len(scr_refs)], refs[g], off, B_WIDTH)
            n_scr[0] += dil > 1
        return job

    exp_jobs = [gate_job(off) for off in range(0, 2 * D_MODEL, SEG)] + [sga_job] + [sgb_job(off) for off in offs]
    norm_jobs = [group_job(p, refs, gain, g, off) for p, (refs, gain) in enumerate(((qb_refs, gqb_ref), (kb_refs, gkb_ref)))
                 for g in range(n_g) for off in offs] + [qa_job]
    plain_jobs = [group_job(2, vb_refs, None, g, off) for g in range(n_g) for off in offs] + [kva_job]
    order = []
    for i in range(max(len(exp_jobs), len(norm_jobs))):
        order += exp_jobs[i:i + 1] + norm_jobs[i:i + 1]
    for job in order + plain_jobs:
        job()


def _in_proj(x2, ng, w, gqa, gka, gqb, gkb, bm, tm):
    m = x2.shape[0]

    def rows(width):
        return pl.BlockSpec((tm, width), lambda i: (i, 0))

    def whole(a):
        return pl.BlockSpec(a.shape, lambda i: (0,) * a.ndim)

    def out(width):
        return jax.ShapeDtypeStruct((m, width), BF16)

    b_shape = [jax.ShapeDtypeStruct((m // dil, dil * B_WIDTH), BF16) for _, dil in B_GROUPS]
    b_spec = [pl.BlockSpec((tm // dil, dil * B_WIDTH), lambda i: (i, 0)) for _, dil in B_GROUPS]
    out_shape = (out(A_WIDTH), out(PAIR), out(PAIR), b_shape, b_shape, b_shape,
                 out(A_WIDTH), out(B_WIDTH), out(2 * D_MODEL))
    out_specs = (rows(A_WIDTH), rows(PAIR), rows(PAIR), b_spec, b_spec, b_spec,
                 rows(A_WIDTH), rows(B_WIDTH), rows(2 * D_MODEL))
    scratch = [[pltpu.VMEM((SEG // LANES, tm, LANES), F32)] * 2]
    return pl.pallas_call(
        _in_proj_kernel,
        grid=(m // tm,),
        in_specs=[rows(D_MODEL), whole(ng), pl.BlockSpec(memory_space=pltpu.VMEM),
                  whole(gqa), whole(gka), whole(gqb), whole(gkb), whole(bm)],
        out_specs=out_specs,
        out_shape=out_shape,
        scratch_shapes=scratch,
        compiler_params=pltpu.CompilerParams(vmem_limit_bytes=VMEM_LIMIT),
        name="in_proj",
    )(x2, ng, w, gqa, gka, gqb, gkb, bm)


def _init_bias_tiles(bias_row_ref, bias_ref, *, half_window, n_heads, stack):
    tk = Q_BLK + 2 * half_window
    width = bias_row_ref.shape[1]
    for li, lead in enumerate((0, half_window, 2 * half_window)):
        shift = width - (Q_BLK - 1 + 2 * half_window - lead)
        for head in range(n_heads):
            base = jnp.broadcast_to(bias_row_ref[head:head + 1, :], (Q_BLK, width))
            tile = pltpu.roll(base, shift, 1, stride=1, stride_axis=0)
            grp, idx = divmod(head, 2 * stack)
            bias_ref[li, grp, idx * Q_BLK:(idx + 1) * Q_BLK, :] = tile[:, :tk]


def _attn_block(blk, load_q, load_kv, bias_ref, sink_ref, store_out, store_stats, *,
                seq_len, half_window, n_cls, q_pairs, kv_pairs, stack):
    tk = Q_BLK + 2 * half_window
    n_blk = seq_len // Q_BLK
    lo = lax.broadcasted_iota(jnp.int32, (1, PAIR), 1) < HEAD_DIM
    lane_l = lax.broadcasted_iota(jnp.int32, (1, LANES), 1)
    ws = pl.multiple_of(jnp.clip(blk * Q_BLK - half_window, 0, seq_len - tk), HEAD_DIM)
    sel = (blk > 0).astype(jnp.int32) + (blk == n_blk - 1).astype(jnp.int32)
    ones = jnp.ones((tk, PAIR), BF16)
    for c in range(n_cls):
        st_tile = jnp.where(lane_l < HEAD_DIM, jnp.zeros((Q_BLK, LANES), F32), 1.0)
        for grp in range(q_pairs // stack):
            k, v = load_kv(ws, (c * kv_pairs + grp * stack * kv_pairs // q_pairs) * PAIR)
            qcols = [(c * q_pairs + grp * stack + i) * PAIR for i in range(stack)]
            qs = []
            for qcol in qcols:
                q = load_q(qcol)
                zero = jnp.zeros_like(q)
                qs += [jnp.where(lo, q, zero), jnp.where(lo, zero, q)]
            s = lax.dot_general(jnp.concatenate(qs, axis=0), k, (((1,), (1,)), ((), ())),
                                preferred_element_type=F32)
            s = s + bias_ref[sel, grp]
            heads = range(2 * stack * grp, 2 * stack * (grp + 1))
            ms, es = [], []
            for i, head in enumerate(heads):
                s_h = s[i * Q_BLK:(i + 1) * Q_BLK]
                m_h = jnp.max(s_h, axis=-1, keepdims=True)
                if sink_ref is not None:
                    m_h = jnp.maximum(m_h, sink_ref[head])
                ms.append(m_h)
                es.append(jnp.exp2(s_h - m_h).astype(BF16))
            pv = jnp.dot(jnp.concatenate(es, axis=0), jnp.concatenate([v, ones], axis=1),
                         preferred_element_type=F32)
            outs = []
            for i, head in enumerate(heads):
                l_h = pv[i * Q_BLK:(i + 1) * Q_BLK, PAIR:]
                if sink_ref is not None:
                    l_h = l_h + jnp.exp2(sink_ref[head] - ms[i])
                if store_stats is not None:
                    outs.append(pv[i * Q_BLK:(i + 1) * Q_BLK, :PAIR])
                    st_tile = jnp.where(lane_l == head, ms[i], st_tile)
                    st_tile = jnp.where(lane_l == HEAD_DIM + head, l_h, st_tile)
                else:
                    outs.append(pv[i * Q_BLK:(i + 1) * Q_BLK, :PAIR] * (1.0 / l_h))
            for i, qcol in enumerate(qcols):
                store_out(qcol, jnp.where(lo, outs[2 * i], outs[2 * i + 1]))
        if store_stats is not None:
            store_stats(c, st_tile)


def _band_attn_kernel(q_ref, k_ref, v_ref, bias_row_ref, o_ref, st_ref, bias_ref, *,
                      seq_len, tq, half_window, n_cls, q_pairs, unroll):
    t = pl.program_id(2)

    @pl.when((pl.program_id(0) == 0) & (pl.program_id(1) == 0) & (t == 0))
    def _():
        _init_bias_tiles(bias_row_ref, bias_ref, half_window=half_window, n_heads=2 * q_pairs, stack=1)

    tk = Q_BLK + 2 * half_window

    def block(j, carry):
        row = pl.multiple_of(j * Q_BLK, Q_BLK)

        def store_out(col, x):
            o_ref[0, pl.ds(row, Q_BLK), col:col + PAIR] = x.astype(BF16)

        def store_stats(c, tile):
            st_ref[0, pl.ds(row, Q_BLK), c * LANES:(c + 1) * LANES] = tile

        _attn_block(t * (tq // Q_BLK) + j,
                    lambda col: q_ref[0, pl.ds(row, Q_BLK), col:col + PAIR],
                    lambda ws, col: (k_ref[0, pl.ds(ws, tk), col:col + PAIR],
                                     v_ref[0, pl.ds(ws, tk), col:col + PAIR]),
                    bias_ref, None, store_out, store_stats, seq_len=seq_len, half_window=half_window,
                    n_cls=n_cls, q_pairs=q_pairs, kv_pairs=q_pairs, stack=1)
        return carry

    lax.fori_loop(0, tq // Q_BLK, block, 0, unroll=unroll)


def _band_attn(q, k, v, bias_rows, *, seq_len, tq, half_window, n_cls, unroll):
    n_rows, _, cols = q.shape
    tk = Q_BLK + 2 * half_window
    q_pairs = bias_rows.shape[0] // 2
    n_col = cols // (q_pairs * PAIR)
    assert k.shape == q.shape and v.shape == q.shape and bias_rows.shape[1] % LANES == 0
    assert seq_len % tq == 0 and tq % Q_BLK == 0 and n_col % n_cls == 0

    blk = n_cls * q_pairs * PAIR
    rows_spec = pl.BlockSpec((1, tq, blk), lambda r, c, t: (r, t, c))
    seq_spec = pl.BlockSpec((1, seq_len, blk), lambda r, c, t: (r, 0, c))
    kern = functools.partial(_band_attn_kernel, seq_len=seq_len, tq=tq, half_window=half_window,
                             n_cls=n_cls, q_pairs=q_pairs, unroll=unroll)
    return pl.pallas_call(
        kern,
        grid=(n_rows, n_col // n_cls, seq_len // tq),
        in_specs=[rows_spec, seq_spec, seq_spec, pl.BlockSpec(bias_rows.shape, lambda r, c, t: (0, 0))],
        out_specs=[rows_spec, pl.BlockSpec((1, tq, n_cls * LANES), lambda r, c, t: (r, t, c))],
        out_shape=[jax.ShapeDtypeStruct(q.shape, BF16),
                   jax.ShapeDtypeStruct((n_rows, seq_len, n_col * LANES), F32)],
        scratch_shapes=[pltpu.VMEM((3, q_pairs, 2 * Q_BLK, tk), F32)],
        compiler_params=pltpu.CompilerParams(vmem_limit_bytes=VMEM_LIMIT,
                                             dimension_semantics=("arbitrary",) * 3),
        name=f"band_attn_len{seq_len}",
    )(q, k, v, bias_rows)


def _merge_out_kernel(x_ref, qa_ref, ka_ref, va_ref, bias_row_ref, sink_ref, sga_ref, ob_refs, st_refs,
                      sgb_ref, gate_ref, expand_ref, wa_ref, wb_ref, wo_ref, y_ref,
                      bias_ref, ya_ref, scr_o_refs, scr_l_refs, *, seq_len):
    i = pl.program_id(0)
    tm = x_ref.shape[0]
    stack = A_Q_HEADS // 2

    @pl.when(i == 0)
    def _():
        _init_bias_tiles(bias_row_ref, bias_ref, half_window=A_WINDOW, n_heads=A_Q_HEADS, stack=stack)

    first_blk = lax.rem(i, seq_len // tm) * (tm // Q_BLK)
    tk = Q_BLK + 2 * A_WINDOW

    def mixer_a_block(j):
        rows = slice(j * Q_BLK, (j + 1) * Q_BLK)

        def store_out(col, x):
            ya_ref[rows, col:col + PAIR] = (x * sga_ref[rows, col:col + PAIR].astype(F32)).astype(BF16)

        _attn_block(first_blk + j,
                    lambda col: qa_ref[rows, col:col + PAIR],
                    lambda ws, col: (ka_ref[0, pl.ds(ws, tk), col:col + PAIR],
                                     va_ref[0, pl.ds(ws, tk), col:col + PAIR]),
                    bias_ref, sink_ref, store_out, None, seq_len=seq_len, half_window=A_WINDOW, n_cls=1,
                    q_pairs=stack, kv_pairs=A_KV_HEADS // 2, stack=stack)

    dils = [dil for _, dil in B_GROUPS]

    def group_weights():
        stats = [_load_by_class(r, dil, s, LANES) for r, dil, s in zip(st_refs, dils, scr_l_refs)]
        is_head = lax.broadcasted_iota(jnp.int32, (1, LANES), 1) < HEAD_DIM
        m = functools.reduce(jnp.maximum, stats)
        es = [jnp.exp2(st - m) for st in stats]
        den = functools.reduce(lambda a, b: a + b, [e * pltpu.roll(st, HEAD_DIM, 1) for e, st in zip(es, stats)])
        inv = jnp.where(is_head, 1.0 / den, 0.0)
        return jnp.concatenate([(e * inv).astype(BF16) for e in es], axis=0)

    def mixer_b_branch(weights):
        alphas = jnp.dot(weights, expand_ref[...], preferred_element_type=F32)
        yb = None
        for g, (o_ref, dil, scr) in enumerate(zip(ob_refs, dils, scr_o_refs)):
            term = alphas[g * tm:(g + 1) * tm] * _load_by_class(o_ref, dil, scr, B_WIDTH)
            yb = term if yb is None else yb + term
        return (yb * sgb_ref[...].astype(F32)).astype(BF16)

    weights = group_weights()
    for j in range(tm // Q_BLK):
        mixer_a_block(j)
    yb = mixer_b_branch(weights)
    br_a = jnp.dot(ya_ref[...], wa_ref[...], preferred_element_type=F32)
    br_b = jnp.dot(yb, wb_ref[...], preferred_element_type=F32)
    merged = (gate_ref[:, :D_MODEL].astype(F32) * br_a
              + gate_ref[:, D_MODEL:].astype(F32) * br_b).astype(BF16)
    y_ref[...] = x_ref[...] + jnp.dot(merged, wo_ref[...], preferred_element_type=F32)


def _merge_out(x2, qa, ka, va, bias_rows, sink, sga, obs, stats, sgb, gates, expand, wa, wb, wo, *, tm, seq_len):
    m = x2.shape[0]
    tiles_per_seq = seq_len // tm

    def rows(width):
        return pl.BlockSpec((tm, width), lambda i: (i, 0))

    def whole(a):
        return pl.BlockSpec(a.shape, lambda i: (0,) * a.ndim)

    def by_class(width):
        return [pl.BlockSpec((tm // dil, dil * width), lambda i: (i, 0)) for _, dil in B_GROUPS]

    seq_kv = pl.BlockSpec((1, seq_len, PAIR), lambda i: (i // tiles_per_seq, 0, 0))
    n_g = len(B_GROUPS)
    stack = A_Q_HEADS // 2
    scratch = [pltpu.VMEM((3, 1, 2 * stack * Q_BLK, Q_BLK + 2 * A_WINDOW), F32),
               pltpu.VMEM((tm, A_WIDTH), BF16),
               [pltpu.VMEM((B_WIDTH // LANES, tm, LANES), F32)] * n_g,
               [pltpu.VMEM((1, tm, LANES), F32)] * n_g]
    return pl.pallas_call(
        functools.partial(_merge_out_kernel, seq_len=seq_len),
        grid=(m // tm,),
        in_specs=[rows(D_MODEL), rows(A_WIDTH), seq_kv, seq_kv, whole(bias_rows),
                  pl.BlockSpec(memory_space=pltpu.SMEM), rows(A_WIDTH), by_class(B_WIDTH),
                  by_class(LANES), rows(B_WIDTH), rows(2 * D_MODEL),
                  whole(expand), whole(wa), whole(wb), whole(wo)],
        out_specs=rows(D_MODEL),
        out_shape=jax.ShapeDtypeStruct((m, D_MODEL), F32),
        scratch_shapes=scratch,
        compiler_params=pltpu.CompilerParams(vmem_limit_bytes=VMEM_LIMIT, dimension_semantics=("arbitrary",)),
        name="merge_out",
    )(x2, qa, ka, va, bias_rows, sink.astype(F32), sga, obs, stats, sgb, gates, expand, wa, wb, wo)


def _t5_bucket(rel):
    half = N_BUCKETS // 2
    max_exact = half // 2
    ret = (rel > 0).astype(jnp.int32) * half
    n = jnp.abs(rel)
    nf = jnp.maximum(n, max_exact).astype(jnp.float32)
    large = max_exact + jnp.floor(jnp.log(nf / max_exact) / math.log(MAX_DISTANCE / max_exact)
                                  * (half - max_exact)).astype(jnp.int32)
    large = jnp.minimum(large, half - 1)
    return ret + jnp.where(n < max_exact, n, large)


def _bias_rows(table, half_window, stride):
    n_rel = 2 * Q_BLK - 1 + 4 * half_window
    width = -(-n_rel // LANES) * LANES
    rel = jnp.arange(width) - (Q_BLK - 1 + 2 * half_window)
    bucket = _t5_bucket(rel * stride)[None, :]
    tab = table.astype(F32) * LOG2E
    rows = jnp.zeros((table.shape[1], width), F32)
    for b in range(N_BUCKETS):
        rows = jnp.where(bucket == b, tab[b][:, None], rows)
    return jnp.where((jnp.abs(rel) <= half_window)[None, :], rows, NEG_INF)


_A_HEAD_ORDER = tuple(h for p in range(A_Q_HEADS // 2) for h in (p, p + A_Q_HEADS // 2))


def kernel(x, norm_gain, w_in, q_norm_a, k_norm_a, q_norm_b, k_norm_b, sink_a, rel_bias,
           w_branch_a, w_branch_b, b_merge, w_out):
    bsz, seq, d = x.shape
    scale = HEAD_DIM ** -0.5

    expand = jnp.asarray(np.kron(np.eye(LANES, B_HEADS), np.ones((1, HEAD_DIM))), BF16)
    for layer in range(norm_gain.shape[0]):
        w = w_in[layer].astype(BF16)
        wa = jnp.concatenate([w_branch_a[layer][h * HEAD_DIM:(h + 1) * HEAD_DIM] for h in _A_HEAD_ORDER],
                             axis=0).astype(BF16)
        wb = w_branch_b[layer].astype(BF16)
        wo = w_out[layer].astype(BF16)
        gqa = jnp.tile(q_norm_a[layer] * (scale * LOG2E), A_WIDTH // HEAD_DIM)[None]
        gka = jnp.tile(k_norm_a[layer], A_KV_HEADS)[None]
        gqb = jnp.tile(q_norm_b[layer] * (scale * LOG2E), B_HEADS)[None]
        gkb = jnp.tile(k_norm_b[layer], B_HEADS)[None]
        bm = b_merge[layer].reshape(1, -1)
        sink = sink_a[layer][np.asarray(_A_HEAD_ORDER)] * LOG2E

        x2 = x.reshape(bsz * seq, d)
        qa, ka, va, qbs, kbs, vbs, sga, sgb, gates = _in_proj(
            x2, norm_gain[layer][None], w, gqa, gka, gqb, gkb, bm, tm=ROW_TILE)

        obs, stats = [], []
        for g, (window, dil) in enumerate(B_GROUPS):
            c0 = A_Q_HEADS + g * B_HEADS
            hw = window // (2 * dil)
            sub = seq // dil
            bias_g = _bias_rows(rel_bias[:, c0:c0 + B_HEADS], hw, dil)
            view = (bsz, sub, dil * B_WIDTH)
            tq = min(sub, B_ROWS_PER_STEP)
            n_cls = min(dil, B_ROWS_PER_STEP // tq)
            unroll = min(tq // Q_BLK, B_UNITS_PER_BODY // (n_cls * B_HEADS))
            o_g, st_g = _band_attn(qbs[g].reshape(view), kbs[g].reshape(view), vbs[g].reshape(view), bias_g,
                                   seq_len=sub, tq=tq, half_window=hw, n_cls=n_cls, unroll=unroll)
            obs.append(o_g.reshape(bsz * sub, dil * B_WIDTH))
            stats.append(st_g.reshape(bsz * sub, dil * LANES))

        bias_a = _bias_rows(rel_bias[:, :A_Q_HEADS][:, np.asarray(_A_HEAD_ORDER)], A_WINDOW, 1)
        y2 = _merge_out(x2, qa, ka.reshape(bsz, seq, PAIR), va.reshape(bsz, seq, PAIR), bias_a, sink,
                        sga, obs, stats, sgb, gates, expand, wa, wb, wo, tm=ROW_TILE, seq_len=seq)
        x = y2.reshape(bsz, seq, d)
    return x
```

```python
import functools
import math

import jax
import jax.numpy as jnp
import numpy as np
from jax import lax
from jax.experimental import pallas as pl
from jax.experimental.pallas import tpu as pltpu

D_MODEL = 1024
HEAD_DIM = 64
A_Q_HEADS = 8
A_KV_HEADS = 2
A_WINDOW = 128
B_GROUPS = ((128, 1), (512, 4), (2048, 16))
B_HEADS = 8
A_WIDTH = A_Q_HEADS * HEAD_DIM
B_WIDTH = B_HEADS * HEAD_DIM
N_BUCKETS = 32
MAX_DISTANCE = 1024
EPS = 1e-6
NEG_INF = -1e30
LOG2E = 1.0 / math.log(2.0)

LANES = 128
PAIR = 2 * HEAD_DIM
Q_BLK = 128
SEG = 256
ROW_TILE = 512
MERGE_SEG = 512
B_ROWS_PER_STEP = 2048
B_UNITS_PER_BODY = 128
VMEM_LIMIT = 56 * 1024 * 1024

BF16 = jnp.bfloat16
F32 = jnp.float32


def _head_norm(y, gain):
    lo = lax.broadcasted_iota(jnp.int32, (1, PAIR), 1) < HEAD_DIM
    parts = []
    for c in range(0, y.shape[1], PAIR):
        yc = y[:, c:c + PAIR]
        sq = yc * yc
        s_lo = jnp.sum(jnp.where(lo, sq, 0.0), axis=-1, keepdims=True)
        s_hi = jnp.sum(jnp.where(lo, 0.0, sq), axis=-1, keepdims=True)
        ms = jnp.where(lo, s_lo, s_hi) * (1.0 / HEAD_DIM)
        parts.append(yc * lax.rsqrt(ms + EPS))
    yn = parts[0] if len(parts) == 1 else jnp.concatenate(parts, axis=1)
    return yn * gain


def _silu(y):
    return y * (1.0 / (1.0 + jnp.exp(-y)))


def _pair_order(y):
    lo = lax.broadcasted_iota(jnp.int32, (1, PAIR), 1) < HEAD_DIM
    n_pairs = A_Q_HEADS // 2
    cols = [y[:, g * PAIR:(g + 1) * PAIR] for g in range(n_pairs)]
    swapped = [pltpu.roll(col, HEAD_DIM, 1) for col in cols]
    out = []
    for p in range(n_pairs):
        first, second = p, p + n_pairs
        lo_src = cols[first // 2] if first % 2 == 0 else swapped[first // 2]
        hi_src = cols[second // 2] if second % 2 == 1 else swapped[second // 2]
        out.append(jnp.where(lo, lo_src, hi_src))
    return jnp.concatenate(out, axis=1)


def _store_by_class(y, dil, scr_ref, out_ref, col0, width):
    rows, w = y.shape
    if dil == 1:
        out_ref[:, col0:col0 + w] = y.astype(out_ref.dtype)
        return
    slabs = w // LANES
    for s in range(slabs):
        scr_ref[s] = y[:, s * LANES:(s + 1) * LANES]
    for c in range(dil):
        for s in range(slabs):
            col = c * width + col0 + s * LANES
            out_ref[:, col:col + LANES] = scr_ref[s, pl.ds(c, rows // dil, stride=dil), :].astype(out_ref.dtype)


def _load_by_class(ref, dil, scr_ref, width):
    if dil == 1:
        return ref[...].astype(F32)
    sub_rows = ref.shape[0]
    slabs = width // LANES
    for c in range(dil):
        for s in range(slabs):
            col = c * width + s * LANES
            scr_ref[s, pl.ds(c, sub_rows, stride=dil), :] = ref[:, col:col + LANES].astype(F32)
    parts = [scr_ref[s] for s in range(slabs)]
    return parts[0] if slabs == 1 else jnp.concatenate(parts, axis=1)


def _in_proj_kernel(x_ref, ng_ref, w_ref, gqa_ref, gka_ref, gqb_ref, gkb_ref, bm_ref,
                    qa_ref, ka_ref, va_ref, qb_refs, kb_refs, vb_refs, sga_ref, sgb_ref, gate_ref,
                    scr_refs):
    x = x_ref[...]
    ms = jnp.mean(x * x, axis=-1, keepdims=True)
    h = ((x * lax.rsqrt(ms + EPS)) * ng_ref[...]).astype(BF16)

    def proj(c0):
        return jnp.dot(h, w_ref[:, c0:c0 + SEG], preferred_element_type=F32)

    n_g = len(B_GROUPS)
    c_kva = A_WIDTH
    c_b = c_kva + 2 * PAIR
    c_ga = c_b + 3 * n_g * B_WIDTH
    c_gb = c_ga + A_WIDTH
    c_mg = c_gb + B_WIDTH
    offs = range(0, B_WIDTH, SEG)

    def gate_job(off):
        def job():
            y = proj(c_mg + off) + bm_ref[:, off:off + SEG]
            gate_ref[:, off:off + SEG] = (1.0 / (1.0 + jnp.exp(-y))).astype(BF16)
        return job

    def sgb_job(off):
        def job():
            sgb_ref[:, off:off + SEG] = _silu(proj(c_gb + off)).astype(BF16)
        return job

    def sga_job():
        sga = [_silu(proj(c_ga + off)) for off in range(0, A_WIDTH, SEG)]
        sga_ref[...] = _pair_order(jnp.concatenate(sga, axis=1)).astype(BF16)

    def qa_job():
        qa = [_head_norm(proj(off), gqa_ref[:, off:off + SEG]) for off in range(0, A_WIDTH, SEG)]
        qa_ref[...] = _pair_order(jnp.concatenate(qa, axis=1)).astype(BF16)

    def kva_job():
        y = proj(c_kva)
        ka_ref[...] = _head_norm(y[:, :PAIR], gka_ref[...]).astype(BF16)
        va_ref[...] = y[:, PAIR:].astype(BF16)

    n_scr = [0]

    def group_job(part, refs, gain_ref, g, off):
        def job():
            dil = B_GROUPS[g][1]
            y = proj(c_b + (part * n_g + g) * B_WIDTH + off)
            if gain_ref is not None:
                y = _head_norm(y, gain_ref[:, off:off + SEG])
            _store_by_class(y, dil, scr_refs[n_scr[0] % len(scr_refs)], refs[g], off, B_WIDTH)
            n_scr[0] += dil > 1
        return job

    exp_jobs = [gate_job(off) for off in range(0, 2 * D_MODEL, SEG)] + [sga_job] + [sgb_job(off) for off in offs]
    norm_jobs = [group_job(p, refs, gain, g, off) for p, (refs, gain) in enumerate(((qb_refs, gqb_ref), (kb_refs, gkb_ref)))
                 for g in range(n_g) for off in offs] + [qa_job]
    plain_jobs = [group_job(2, vb_refs, None, g, off) for g in range(n_g) for off in offs] + [kva_job]
    order = []
    for i in range(max(len(exp_jobs), len(norm_jobs))):
        order += exp_jobs[i:i + 1] + norm_jobs[i:i + 1]
    for job in order + plain_jobs:
        job()


def _in_proj(x2, ng, w, gqa, gka, gqb, gkb, bm, tm):
    m = x2.shape[0]

    def rows(width):
        return pl.BlockSpec((tm, width), lambda i: (i, 0))

    def whole(a):
        return pl.BlockSpec(a.shape, lambda i: (0,) * a.ndim)

    def out(width):
        return jax.ShapeDtypeStruct((m, width), BF16)

    b_shape = [jax.ShapeDtypeStruct((m // dil, dil * B_WIDTH), BF16) for _, dil in B_GROUPS]
    b_spec = [pl.BlockSpec((tm // dil, dil * B_WIDTH), lambda i: (i, 0)) for _, dil in B_GROUPS]
    out_shape = (out(A_WIDTH), out(PAIR), out(PAIR), b_shape, b_shape, b_shape,
                 out(A_WIDTH), out(B_WIDTH), out(2 * D_MODEL))
    out_specs = (rows(A_WIDTH), rows(PAIR), rows(PAIR), b_spec, b_spec, b_spec,
                 rows(A_WIDTH), rows(B_WIDTH), rows(2 * D_MODEL))
    scratch = [[pltpu.VMEM((SEG // LANES, tm, LANES), F32)] * 2]
    return pl.pallas_call(
        _in_proj_kernel,
        grid=(m // tm,),
        in_specs=[rows(D_MODEL), whole(ng), pl.BlockSpec(memory_space=pltpu.VMEM),
                  whole(gqa), whole(gka), whole(gqb), whole(gkb), whole(bm)],
        out_specs=out_specs,
        out_shape=out_shape,
        scratch_shapes=scratch,
        compiler_params=pltpu.CompilerParams(vmem_limit_bytes=VMEM_LIMIT),
        name="in_proj",
    )(x2, ng, w, gqa, gka, gqb, gkb, bm)


def _init_bias_tiles(bias_row_ref, bias_ref, *, half_window, n_heads, stack):
    tk = Q_BLK + 2 * half_window
    width = bias_row_ref.shape[1]
    for li, lead in enumerate((0, half_window, 2 * half_window)):
        shift = width - (Q_BLK - 1 + 2 * half_window - lead)
        for head in range(n_heads):
            base = jnp.broadcast_to(bias_row_ref[head:head + 1, :], (Q_BLK, width))
            tile = pltpu.roll(base, shift, 1, stride=1, stride_axis=0)
            grp, idx = divmod(head, 2 * stack)
            bias_ref[li, grp, idx * Q_BLK:(idx + 1) * Q_BLK, :] = tile[:, :tk]


def _attn_block(blk, load_q, load_kv, bias_ref, sink_ref, store_out, store_stats, *,
                seq_len, half_window, n_cls, q_pairs, kv_pairs, stack):
    tk = Q_BLK + 2 * half_window
    n_blk = seq_len // Q_BLK
    lo = lax.broadcasted_iota(jnp.int32, (1, PAIR), 1) < HEAD_DIM
    lane_l = lax.broadcasted_iota(jnp.int32, (1, LANES), 1)
    ws = pl.multiple_of(jnp.clip(blk * Q_BLK - half_window, 0, seq_len - tk), HEAD_DIM)
    sel = (blk > 0).astype(jnp.int32) + (blk == n_blk - 1).astype(jnp.int32)
    ones = jnp.ones((tk, PAIR), BF16)
    for c in range(n_cls):
        st_tile = jnp.where(lane_l < HEAD_DIM, jnp.zeros((Q_BLK, LANES), F32), 1.0)
        for grp in range(q_pairs // stack):
            k, v = load_kv(ws, (c * kv_pairs + grp * stack * kv_pairs // q_pairs) * PAIR)
            qcols = [(c * q_pairs + grp * stack + i) * PAIR for i in range(stack)]
            qs = []
            for qcol in qcols:
                q = load_q(qcol)
                zero = jnp.zeros_like(q)
                qs += [jnp.where(lo, q, zero), jnp.where(lo, zero, q)]
            s = lax.dot_general(jnp.concatenate(qs, axis=0), k, (((1,), (1,)), ((), ())),
                                preferred_element_type=F32)
            s = s + bias_ref[sel, grp]
            heads = range(2 * stack * grp, 2 * stack * (grp + 1))
            ms, es = [], []
            for i, head in enumerate(heads):
                s_h = s[i * Q_BLK:(i + 1) * Q_BLK]
                m_h = jnp.max(s_h, axis=-1, keepdims=True)
                if sink_ref is not None:
                    m_h = jnp.maximum(m_h, sink_ref[head])
                ms.append(m_h)
                es.append(jnp.exp2(s_h - m_h).astype(BF16))
            pv = jnp.dot(jnp.concatenate(es, axis=0), jnp.concatenate([v, ones], axis=1),
                         preferred_element_type=F32)
            outs = []
            for i, head in enumerate(heads):
                l_h = pv[i * Q_BLK:(i + 1) * Q_BLK, PAIR:]
                if sink_ref is not None:
                    l_h = l_h + jnp.exp2(sink_ref[head] - ms[i])
                if store_stats is not None:
                    outs.append(pv[i * Q_BLK:(i + 1) * Q_BLK, :PAIR])
                    st_tile = jnp.where(lane_l == head, ms[i], st_tile)
                    st_tile = jnp.where(lane_l == HEAD_DIM + head, l_h, st_tile)
                else:
                    outs.append(pv[i * Q_BLK:(i + 1) * Q_BLK, :PAIR] * (1.0 / l_h))
            for i, qcol in enumerate(qcols):
                store_out(qcol, jnp.where(lo, outs[2 * i], outs[2 * i + 1]))
        if store_stats is not None:
            store_stats(c, st_tile)


def _band_attn_kernel(q_ref, k_ref, v_ref, bias_row_ref, o_ref, st_ref, bias_ref, *,
                      seq_len, tq, half_window, n_cls, q_pairs, unroll):
    t = pl.program_id(2)

    @pl.when((pl.program_id(0) == 0) & (pl.program_id(1) == 0) & (t == 0))
    def _():
        _init_bias_tiles(bias_row_ref, bias_ref, half_window=half_window, n_heads=2 * q_pairs, stack=1)

    tk = Q_BLK + 2 * half_window

    def block(j, carry):
        row = pl.multiple_of(j * Q_BLK, Q_BLK)

        def store_out(col, x):
            o_ref[0, pl.ds(row, Q_BLK), col:col + PAIR] = x.astype(BF16)

        def store_stats(c, tile):
            st_ref[0, pl.ds(row, Q_BLK), c * LANES:(c + 1) * LANES] = tile

        _attn_block(t * (tq // Q_BLK) + j,
                    lambda col: q_ref[0, pl.ds(row, Q_BLK), col:col + PAIR],
                    lambda ws, col: (k_ref[0, pl.ds(ws, tk), col:col + PAIR],
                                     v_ref[0, pl.ds(ws, tk), col:col + PAIR]),
                    bias_ref, None, store_out, store_stats, seq_len=seq_len, half_window=half_window,
                    n_cls=n_cls, q_pairs=q_pairs, kv_pairs=q_pairs, stack=1)
        return carry

    lax.fori_loop(0, tq // Q_BLK, block, 0, unroll=unroll)


def _band_attn(q, k, v, bias_rows, *, seq_len, tq, half_window, n_cls, unroll):
    n_rows, _, cols = q.shape
    tk = Q_BLK + 2 * half_window
    q_pairs = bias_rows.shape[0] // 2
    n_col = cols // (q_pairs * PAIR)
    assert k.shape == q.shape and v.shape == q.shape and bias_rows.shape[1] % LANES == 0
    assert seq_len % tq == 0 and tq % Q_BLK == 0 and n_col % n_cls == 0

    blk = n_cls * q_pairs * PAIR
    rows_spec = pl.BlockSpec((1, tq, blk), lambda r, c, t: (r, t, c))
    seq_spec = pl.BlockSpec((1, seq_len, blk), lambda r, c, t: (r, 0, c))
    kern = functools.partial(_band_attn_kernel, seq_len=seq_len, tq=tq, half_window=half_window,
                             n_cls=n_cls, q_pairs=q_pairs, unroll=unroll)
    return pl.pallas_call(
        kern,
        grid=(n_rows, n_col // n_cls, seq_len // tq),
        in_specs=[rows_spec, seq_spec, seq_spec, pl.BlockSpec(bias_rows.shape, lambda r, c, t: (0, 0))],
        out_specs=[rows_spec, pl.BlockSpec((1, tq, n_cls * LANES), lambda r, c, t: (r, t, c))],
        out_shape=[jax.ShapeDtypeStruct(q.shape, BF16),
                   jax.ShapeDtypeStruct((n_rows, seq_len, n_col * LANES), F32)],
        scratch_shapes=[pltpu.VMEM((3, q_pairs, 2 * Q_BLK, tk), F32)],
        compiler_params=pltpu.CompilerParams(vmem_limit_bytes=VMEM_LIMIT,
                                             dimension_semantics=("arbitrary",) * 3),
        name=f"band_attn_len{seq_len}",
    )(q, k, v, bias_rows)


def _merge_out_kernel(x_ref, qa_ref, ka_ref, va_ref, bias_row_ref, sink_ref, sga_ref, ob_refs, st_refs,
                      sgb_ref, gate_ref, expand_ref, wa_ref, wb_ref, wo_ref, y_ref,
                      bias_ref, ya_ref, scr_o_refs, scr_l_refs, *, seq_len):
    i = pl.program_id(0)
    tm = x_ref.shape[0]
    stack = A_Q_HEADS // 2

    @pl.when(i == 0)
    def _():
        _init_bias_tiles(bias_row_ref, bias_ref, half_window=A_WINDOW, n_heads=A_Q_HEADS, stack=stack)

    first_blk = lax.rem(i, seq_len // tm) * (tm // Q_BLK)
    tk = Q_BLK + 2 * A_WINDOW

    def mixer_a_block(j):
        rows = slice(j * Q_BLK, (j + 1) * Q_BLK)

        def store_out(col, x):
            ya_ref[rows, col:col + PAIR] = (x * sga_ref[rows, col:col + PAIR].astype(F32)).astype(BF16)

        _attn_block(first_blk + j,
                    lambda col: qa_ref[rows, col:col + PAIR],
                    lambda ws, col: (ka_ref[0, pl.ds(ws, tk), col:col + PAIR],
                                     va_ref[0, pl.ds(ws, tk), col:col + PAIR]),
                    bias_ref, sink_ref, store_out, None, seq_len=seq_len, half_window=A_WINDOW, n_cls=1,
                    q_pairs=stack, kv_pairs=A_KV_HEADS // 2, stack=stack)

    dils = [dil for _, dil in B_GROUPS]

    def group_weights():
        stats = [_load_by_class(r, dil, s, LANES) for r, dil, s in zip(st_refs, dils, scr_l_refs)]
        is_head = lax.broadcasted_iota(jnp.int32, (1, LANES), 1) < HEAD_DIM
        m = functools.reduce(jnp.maximum, stats)
        es = [jnp.exp2(st - m) for st in stats]
        den = functools.reduce(lambda a, b: a + b, [e * pltpu.roll(st, HEAD_DIM, 1) for e, st in zip(es, stats)])
        inv = jnp.where(is_head, 1.0 / den, 0.0)
        return jnp.concatenate([(e * inv).astype(BF16) for e in es], axis=0)

    def mixer_b_branch(weights):
        alphas = jnp.dot(weights, expand_ref[...], preferred_element_type=F32)
        yb = None
        for g, (o_ref, dil, scr) in enumerate(zip(ob_refs, dils, scr_o_refs)):
            term = alphas[g * tm:(g + 1) * tm] * _load_by_class(o_ref, dil, scr, B_WIDTH)
            yb = term if yb is None else yb + term
        return (yb * sgb_ref[...].astype(F32)).astype(BF16)

    weights = group_weights()
    for j in range(tm // Q_BLK):
        mixer_a_block(j)
    yb = mixer_b_branch(weights)
    ya = ya_ref[...]
    y = x_ref[...]
    for c0 in range(0, D_MODEL, MERGE_SEG):
        br_a = jnp.dot(ya, wa_ref[:, c0:c0 + MERGE_SEG], preferred_element_type=F32)
        br_b = jnp.dot(yb, wb_ref[:, c0:c0 + MERGE_SEG], preferred_element_type=F32)
        merged = (gate_ref[:, c0:c0 + MERGE_SEG].astype(F32) * br_a
                  + gate_ref[:, D_MODEL + c0:D_MODEL + c0 + MERGE_SEG].astype(F32) * br_b).astype(BF16)
        y = y + jnp.dot(merged, wo_ref[c0:c0 + MERGE_SEG, :], preferred_element_type=F32)
    y_ref[...] = y


def _merge_out(x2, qa, ka, va, bias_rows, sink, sga, obs, stats, sgb, gates, expand, wa, wb, wo, *, tm, seq_len):
    m = x2.shape[0]
    tiles_per_seq = seq_len // tm

    def rows(width):
        return pl.BlockSpec((tm, width), lambda i: (i, 0))

    def whole(a):
        return pl.BlockSpec(a.shape, lambda i: (0,) * a.ndim)

    def by_class(width):
        return [pl.BlockSpec((tm // dil, dil * width), lambda i: (i, 0)) for _, dil in B_GROUPS]

    seq_kv = pl.BlockSpec((1, seq_len, PAIR), lambda i: (i // tiles_per_seq, 0, 0))
    n_g = len(B_GROUPS)
    stack = A_Q_HEADS // 2
    scratch = [pltpu.VMEM((3, 1, 2 * stack * Q_BLK, Q_BLK + 2 * A_WINDOW), F32),
               pltpu.VMEM((tm, A_WIDTH), BF16),
               [pltpu.VMEM((B_WIDTH // LANES, tm, LANES), F32)] * n_g,
               [pltpu.VMEM((1, tm, LANES), F32)] * n_g]
    return pl.pallas_call(
        functools.partial(_merge_out_kernel, seq_len=seq_len),
        grid=(m // tm,),
        in_specs=[rows(D_MODEL), rows(A_WIDTH), seq_kv, seq_kv, whole(bias_rows),
                  pl.BlockSpec(memory_space=pltpu.SMEM), rows(A_WIDTH), by_class(B_WIDTH),
                  by_class(LANES), rows(B_WIDTH), rows(2 * D_MODEL),
                  whole(expand), whole(wa), whole(wb), whole(wo)],
        out_specs=rows(D_MODEL),
        out_shape=jax.ShapeDtypeStruct((m, D_MODEL), F32),
        scratch_shapes=scratch,
        compiler_params=pltpu.CompilerParams(vmem_limit_bytes=VMEM_LIMIT, dimension_semantics=("arbitrary",)),
        name="merge_out",
    )(x2, qa, ka, va, bias_rows, sink.astype(F32), sga, obs, stats, sgb, gates, expand, wa, wb, wo)


def _t5_bucket(rel):
    half = N_BUCKETS // 2
    max_exact = half // 2
    ret = (rel > 0).astype(jnp.int32) * half
    n = jnp.abs(rel)
    nf = jnp.maximum(n, max_exact).astype(jnp.float32)
    large = max_exact + jnp.floor(jnp.log(nf / max_exact) / math.log(MAX_DISTANCE / max_exact)
                                  * (half - max_exact)).astype(jnp.int32)
    large = jnp.minimum(large, half - 1)
    return ret + jnp.where(n < max_exact, n, large)


def _bias_rows(table, half_window, stride):
    n_rel = 2 * Q_BLK - 1 + 4 * half_window
    width = -(-n_rel // LANES) * LANES
    rel = jnp.arange(width) - (Q_BLK - 1 + 2 * half_window)
    bucket = _t5_bucket(rel * stride)[None, :]
    tab = table.astype(F32) * LOG2E
    rows = jnp.zeros((table.shape[1], width), F32)
    for b in range(N_BUCKETS):
        rows = jnp.where(bucket == b, tab[b][:, None], rows)
    return jnp.where((jnp.abs(rel) <= half_window)[None, :], rows, NEG_INF)


_A_HEAD_ORDER = tuple(h for p in range(A_Q_HEADS // 2) for h in (p, p + A_Q_HEADS // 2))


def kernel(x, norm_gain, w_in, q_norm_a, k_norm_a, q_norm_b, k_norm_b, sink_a, rel_bias,
           w_branch_a, w_branch_b, b_merge, w_out):
    bsz, seq, d = x.shape
    scale = HEAD_DIM ** -0.5

    expand = jnp.asarray(np.kron(np.eye(LANES, B_HEADS), np.ones((1, HEAD_DIM))), BF16)
    for layer in range(norm_gain.shape[0]):
        w = w_in[layer].astype(BF16)
        wa = jnp.concatenate([w_branch_a[layer][h * HEAD_DIM:(h + 1) * HEAD_DIM] for h in _A_HEAD_ORDER],
                             axis=0).astype(BF16)
        wb = w_branch_b[layer].astype(BF16)
        wo = w_out[layer].astype(BF16)
        gqa = jnp.tile(q_norm_a[layer] * (scale * LOG2E), A_WIDTH // HEAD_DIM)[None]
        gka = jnp.tile(k_norm_a[layer], A_KV_HEADS)[None]
        gqb = jnp.tile(q_norm_b[layer] * (scale * LOG2E), B_HEADS)[None]
        gkb = jnp.tile(k_norm_b[layer], B_HEADS)[None]
        bm = b_merge[layer].reshape(1, -1)
        sink = sink_a[layer][np.asarray(_A_HEAD_ORDER)] * LOG2E

        x2 = x.reshape(bsz * seq, d)
        qa, ka, va, qbs, kbs, vbs, sga, sgb, gates = _in_proj(
            x2, norm_gain[layer][None], w, gqa, gka, gqb, gkb, bm, tm=ROW_TILE)

        obs, stats = [], []
        for g, (window, dil) in enumerate(B_GROUPS):
            c0 = A_Q_HEADS + g * B_HEADS
            hw = window // (2 * dil)
            sub = seq // dil
            bias_g = _bias_rows(rel_bias[:, c0:c0 + B_HEADS], hw, dil)
            view = (bsz, sub, dil * B_WIDTH)
            tq = min(sub, B_ROWS_PER_STEP)
            n_cls = min(dil, B_ROWS_PER_STEP // tq)
            unroll = min(tq // Q_BLK, B_UNITS_PER_BODY // (n_cls * B_HEADS))
            o_g, st_g = _band_attn(qbs[g].reshape(view), kbs[g].reshape(view), vbs[g].reshape(view), bias_g,
                                   seq_len=sub, tq=tq, half_window=hw, n_cls=n_cls, unroll=unroll)
            obs.append(o_g.reshape(bsz * sub, dil * B_WIDTH))
            stats.append(st_g.reshape(bsz * sub, dil * LANES))

        bias_a = _bias_rows(rel_bias[:, :A_Q_HEADS][:, np.asarray(_A_HEAD_ORDER)], A_WINDOW, 1)
        y2 = _merge_out(x2, qa, ka.reshape(bsz, seq, PAIR), va.reshape(bsz, seq, PAIR), bias_a, sink,
                        sga, obs, stats, sgb, gates, expand, wa, wb, wo, tm=ROW_TILE, seq_len=seq)
        x = y2.reshape(bsz, seq, d)
    return x
```
